```python
import functools
import jax, jax.numpy as jnp
from jax import lax
import numpy as np

D_MODEL = 2048
BATCH = 8
SEQ = 2048
DEPTH = 1
DEC_BATCH = 32
DEC_SEQ = 1
PAST_LEN = 16384
PAGE_SIZE = 128

ATT_HEADS = 8
ATT_HD = 128
ATT_WIDTH = ATT_HEADS * ATT_HD
MOBA_BLOCK = 256
MOBA_TOPK = 3
MOBA_QBLOCK = 16
GLA_HEADS = 4
GLA_DK = 128
GLA_DV = 256
GLA_KW = GLA_HEADS * GLA_DK
GLA_VW = GLA_HEADS * GLA_DV
GLA_GATE_RANK = 16
GLA_TAU = 16.0
GLA_CHUNK = 64
MIX_WIDTH = ATT_WIDTH + GLA_VW
D_FF = -(-8 * D_MODEL // (3 * 256)) * 256
IN_WIDTH = 3 * ATT_WIDTH + 2 * GLA_KW + 2 * GLA_VW + GLA_GATE_RANK
SPLITS = [ATT_WIDTH, 2 * ATT_WIDTH, 3 * ATT_WIDTH,
          3 * ATT_WIDTH + GLA_KW, 3 * ATT_WIDTH + 2 * GLA_KW,
          3 * ATT_WIDTH + 2 * GLA_KW + GLA_VW, 3 * ATT_WIDTH + 2 * GLA_KW + 2 * GLA_VW]
EPS = 1e-6

kernel_name = 'hymba_moba_gla_decoder_step'


def rms_norm(x, g):
    xf = x.astype(jnp.float32)
    y = xf * lax.rsqrt(jnp.mean(xf * xf, axis=-1, keepdims=True) + EPS)
    return (y * g.astype(jnp.float32)).astype(x.dtype)


def alibi_slopes(n):
    return jnp.exp2(-8.0 * jnp.arange(1, n + 1, dtype=jnp.float32) / n)


def moba_core(q, qpos, means, fetch, own_k, own_v, own_pos, slopes):
    B, T, H, hd = q.shape
    nb = means.shape[1]
    n_sel = min(MOBA_TOPK, nb)
    n_full = qpos // MOBA_BLOCK
    gate = jnp.einsum('bthd,bnhd->bthn', q.astype(jnp.float32), means)
    cand = jnp.arange(nb)[None, :] < n_full[:, None]
    gate = jnp.where(cand[None, :, None, :], gate, -jnp.inf)
    _, idx = lax.top_k(gate, n_sel)
    sel_valid = jnp.arange(n_sel)[None, :] < n_full[:, None]
    sel_pos = (idx[..., None] * MOBA_BLOCK + jnp.arange(MOBA_BLOCK)).reshape(B, T, H, n_sel * MOBA_BLOCK)
    k_sel, v_sel = fetch(sel_pos)
    s_sel = jnp.einsum('bthd,bthpd->bthp', q, k_sel).astype(jnp.float32)
    s_sel = s_sel - slopes[None, None, :, None] * (qpos[None, :, None, None] - sel_pos).astype(jnp.float32)
    mask_sel = jnp.repeat(sel_valid, MOBA_BLOCK, axis=-1)[None, :, None, :]
    s_sel = jnp.where(mask_sel, s_sel, -jnp.inf)
    s_own = jnp.einsum('bthd,blhd->bthl', q, own_k).astype(jnp.float32)
    dist_own = (qpos[:, None] - own_pos[None, :]).astype(jnp.float32)
    s_own = s_own - slopes[None, None, :, None] * dist_own[None, :, None, :]
    own_mask = (own_pos[None, :] // MOBA_BLOCK == qpos[:, None] // MOBA_BLOCK) & (own_pos[None, :] <= qpos[:, None])
    s_own = jnp.where(own_mask[None, :, None, :], s_own, -jnp.inf)
    p = jax.nn.softmax(jnp.concatenate([s_sel, s_own], axis=-1), axis=-1)
    n_p = s_sel.shape[-1]
    p_sel = p[..., :n_p].astype(v_sel.dtype)
    p_own = p[..., n_p:].astype(own_v.dtype)
    return (jnp.einsum('bthp,bthpd->bthd', p_sel, v_sel)
            + jnp.einsum('bthl,blhd->bthd', p_own, own_v))


def moba_prompt(q, k, v, slopes):
    B, L, H, hd = q.shape
    nb = -(-L // MOBA_BLOCK)
    pad = nb * MOBA_BLOCK - L
    k_pad = jnp.pad(k, ((0, 0), (0, pad), (0, 0), (0, 0)))
    v_pad = jnp.pad(v, ((0, 0), (0, pad), (0, 0), (0, 0)))
    means = k_pad.astype(jnp.float32).reshape(B, nb, MOBA_BLOCK, H, hd).sum(axis=2) / MOBA_BLOCK
    b_idx = jnp.arange(B)[:, None, None, None]
    h_idx = jnp.arange(H)[None, None, :, None]

    def fetch(pos):
        return k_pad[b_idx, pos, h_idx], v_pad[b_idx, pos, h_idx]

    def one_block(start):
        qb = lax.dynamic_slice_in_dim(q, start, MOBA_QBLOCK, axis=1)
        qpos = start + jnp.arange(MOBA_QBLOCK)
        own0 = (start // MOBA_BLOCK) * MOBA_BLOCK
        own_k = lax.dynamic_slice_in_dim(k_pad, own0, MOBA_BLOCK, axis=1)
        own_v = lax.dynamic_slice_in_dim(v_pad, own0, MOBA_BLOCK, axis=1)
        own_pos = own0 + jnp.arange(MOBA_BLOCK)
        return moba_core(qb, qpos, means, fetch, own_k, own_v, own_pos, slopes)

    out = lax.map(one_block, jnp.arange(0, L, MOBA_QBLOCK))
    return jnp.moveaxis(out, 0, 1).reshape(B, L, H, hd)


def moba_sample(q, k_new, v_new, pool_k, pool_v, page_table, slopes):
    B, T, H, hd = q.shape
    n_pages = page_table.shape[1]
    past = n_pages * PAGE_SIZE
    nb = -(-(past + T) // MOBA_BLOCK)
    page_sums = pool_k[page_table].astype(jnp.float32).sum(axis=2)
    page_blk = (jnp.arange(n_pages) * PAGE_SIZE) // MOBA_BLOCK
    new_blk = (past + jnp.arange(T)) // MOBA_BLOCK
    sums = (jax.ops.segment_sum(jnp.moveaxis(page_sums, 1, 0), page_blk, num_segments=nb)
            + jax.ops.segment_sum(jnp.moveaxis(k_new.astype(jnp.float32), 1, 0), new_blk, num_segments=nb))
    means = jnp.moveaxis(sums / MOBA_BLOCK, 0, 1)
    b_idx = jnp.arange(B)[:, None, None, None]
    h_idx = jnp.arange(H)[None, None, :, None]

    def fetch(pos):
        in_cache = (pos < past)[..., None]
        pc = jnp.minimum(pos, past - 1)
        pages = page_table[b_idx, pc // PAGE_SIZE]
        rows = pc % PAGE_SIZE
        pn = jnp.clip(pos - past, 0, T - 1)
        kk = jnp.where(in_cache, pool_k[pages, rows, h_idx], k_new[b_idx, pn, h_idx])
        vv = jnp.where(in_cache, pool_v[pages, rows, h_idx], v_new[b_idx, pn, h_idx])
        return kk, vv

    own0 = (past // MOBA_BLOCK) * MOBA_BLOCK
    n_own = past - own0 + T
    own_pos = own0 + jnp.arange(n_own)
    ok, ov = fetch(jnp.broadcast_to(own_pos[None, None, None, :], (B, 1, H, n_own)))
    own_k = jnp.transpose(ok[:, 0], (0, 2, 1, 3))
    own_v = jnp.transpose(ov[:, 0], (0, 2, 1, 3))
    qpos = past + jnp.arange(T)
    return moba_core(q, qpos, means, fetch, own_k, own_v, own_pos, slopes)


def gla_chunked(q, k, v, log_a, s0):
    B, L, H, dk = q.shape
    dv = v.shape[-1]
    c = GLA_CHUNK if L % GLA_CHUNK == 0 else L
    n = L // c

    def to_chunks(a):
        return jnp.moveaxis(a.astype(jnp.float32).reshape(B, n, c, H, a.shape[-1]), 1, 0)

    causal = jnp.tril(jnp.ones((c, c), dtype=bool))

    def step(S, inp):
        qc, kc, vc, gc = inp
        b = jnp.cumsum(gc, axis=1)
        o_inter = jnp.einsum('bthk,bhkv->bthv', qc * jnp.exp(b), S)
        diff = b[:, :, None] - b[:, None, :]
        decay = jnp.exp(jnp.where(causal[None, :, :, None, None], diff, -jnp.inf))
        att = jnp.einsum('bthk,bshk,btshk->bths', qc, kc, decay)
        o_intra = jnp.einsum('bths,bshv->bthv', att, vc)
        b_last = b[:, -1]
        S_new = (jnp.exp(b_last)[..., None] * S
                 + jnp.einsum('bshk,bshv->bhkv', kc * jnp.exp(b_last[:, None] - b), vc))
        return S_new, o_inter + o_intra

    S, o = lax.scan(step, s0.astype(jnp.float32), (to_chunks(q), to_chunks(k), to_chunks(v), to_chunks(log_a)))
    o = jnp.moveaxis(o, 0, 1).reshape(B, L, H, dv)
    return o.astype(v.dtype), S.astype(s0.dtype)


def trunk_layer(x, s0, attend, norm_g, w_in, w_gate_up, b_gate, att_out_norm_g, gla_norm_g,
                w_o, ffn_norm_g, w_ffn_gate, w_ffn_up, w_ffn_down):
    B, L, _ = x.shape
    xn = rms_norm(x, norm_g)
    q_a, k_a, v_a, q_g, k_g, v_g, r_g, g_low = jnp.split(xn @ w_in, SPLITS, axis=-1)
    q_a = q_a.reshape(B, L, ATT_HEADS, ATT_HD) * (ATT_HD ** -0.5)
    k_a = k_a.reshape(B, L, ATT_HEADS, ATT_HD)
    v_a = v_a.reshape(B, L, ATT_HEADS, ATT_HD)
    att = attend(q_a, k_a, v_a)
    log_a = jax.nn.log_sigmoid((g_low @ w_gate_up + b_gate).astype(jnp.float32)) / GLA_TAU
    gla_o, s_new = gla_chunked(q_g.reshape(B, L, GLA_HEADS, GLA_DK) * (GLA_DK ** -0.5),
                               k_g.reshape(B, L, GLA_HEADS, GLA_DK),
                               v_g.reshape(B, L, GLA_HEADS, GLA_DV),
                               log_a.reshape(B, L, GLA_HEADS, GLA_DK), s0)
    att_y = rms_norm(att.reshape(B, L, ATT_WIDTH), att_out_norm_g)
    gla_y = (rms_norm(gla_o, gla_norm_g) * jax.nn.silu(r_g.reshape(B, L, GLA_HEADS, GLA_DV))).reshape(B, L, GLA_VW)
    h = x + jnp.concatenate([att_y, gla_y], axis=-1) @ w_o
    hn = rms_norm(h, ffn_norm_g)
    y = h + (jax.nn.silu(hn @ w_ffn_gate) * (hn @ w_ffn_up)) @ w_ffn_down
    return y, k_a, v_a, s_new


def setup_inputs(seed: int = 0) -> dict:
    key = jax.random.key(seed)
    ks = jax.random.split(key, 20)
    n_pages = PAST_LEN // PAGE_SIZE
    n_used = DEC_BATCH * n_pages
    n_pool = (5 * n_used + 3) // 4
    page_table = jax.random.permutation(ks[0], n_pool)[:n_used].reshape(DEC_BATCH, n_pages).astype(jnp.int32)
    nrm = lambda k, shape, s=1.0: jax.random.normal(k, shape, jnp.float32) * s
    gain = lambda k, shape: 1.0 + 0.05 * jax.random.normal(k, shape, jnp.float32)
    return {
        'x_prompt': nrm(ks[1], (BATCH, SEQ, D_MODEL)),
        'x_sample': nrm(ks[2], (DEC_BATCH, DEC_SEQ, D_MODEL)),
        'cache_k': nrm(ks[3], (DEPTH, n_pool, PAGE_SIZE, ATT_HEADS, ATT_HD)),
        'cache_v': nrm(ks[4], (DEPTH, n_pool, PAGE_SIZE, ATT_HEADS, ATT_HD)),
        'state_gla': nrm(ks[5], (DEPTH, DEC_BATCH, GLA_HEADS, GLA_DK, GLA_DV), 0.5),
        'page_table': page_table,
        'attn_norm_g': gain(ks[6], (DEPTH, D_MODEL)),
        'w_in': nrm(ks[7], (DEPTH, D_MODEL, IN_WIDTH), D_MODEL ** -0.5),
        'w_gate_up': nrm(ks[8], (DEPTH, GLA_GATE_RANK, GLA_KW), GLA_GATE_RANK ** -0.5),
        'b_gate': nrm(ks[9], (DEPTH, GLA_KW), 0.1),
        'att_out_norm_g': gain(ks[10], (DEPTH, ATT_WIDTH)),
        'gla_norm_g': gain(ks[11], (DEPTH, GLA_DV)),
        'w_o': nrm(ks[12], (DEPTH, MIX_WIDTH, D_MODEL), MIX_WIDTH ** -0.5),
        'ffn_norm_g': gain(ks[13], (DEPTH, D_MODEL)),
        'w_ffn_gate': nrm(ks[14], (DEPTH, D_MODEL, D_FF), D_MODEL ** -0.5),
        'w_ffn_up': nrm(ks[15], (DEPTH, D_MODEL, D_FF), D_MODEL ** -0.5),
        'w_ffn_down': nrm(ks[16], (DEPTH, D_FF, D_MODEL), D_FF ** -0.5),
        'final_norm_g': gain(ks[17], (D_MODEL,)),
    }


def reference(x_prompt, x_sample, cache_k, cache_v, state_gla, page_table, attn_norm_g, w_in,
              w_gate_up, b_gate, att_out_norm_g, gla_norm_g, w_o, ffn_norm_g, w_ffn_gate,
              w_ffn_up, w_ffn_down, final_norm_g):
    slopes = alibi_slopes(ATT_HEADS)
    yp, ys = x_prompt, x_sample
    kp_l, vp_l, ks_l, vs_l, sp_l, ss_l = [], [], [], [], [], []
    for l in range(DEPTH):
        layer_w = (attn_norm_g[l], w_in[l], w_gate_up[l], b_gate[l], att_out_norm_g[l], gla_norm_g[l],
                   w_o[l], ffn_norm_g[l], w_ffn_gate[l], w_ffn_up[l], w_ffn_down[l])
        s0 = jnp.zeros((yp.shape[0], GLA_HEADS, GLA_DK, GLA_DV), jnp.float32)
        attend_p = functools.partial(moba_prompt, slopes=slopes)
        yp, kp, vp, sp = trunk_layer(yp, s0, attend_p, *layer_w)
        attend_s = functools.partial(moba_sample, pool_k=cache_k[l], pool_v=cache_v[l],
                                     page_table=page_table, slopes=slopes)
        ys, kn, vn, sn = trunk_layer(ys, state_gla[l], attend_s, *layer_w)
        kp_l.append(kp); vp_l.append(vp); sp_l.append(sp)
        ks_l.append(kn); vs_l.append(vn); ss_l.append(sn)
    yp = rms_norm(yp, final_norm_g)
    ys = rms_norm(ys, final_norm_g)
    return (yp, ys, jnp.stack(kp_l), jnp.stack(vp_l), jnp.stack(ks_l), jnp.stack(vs_l),
            jnp.stack(sp_l), jnp.stack(ss_l))
```

```python
import functools

import jax
import jax.numpy as jnp
from jax import lax
from jax.experimental import pallas as pl
from jax.experimental.pallas import tpu as pltpu

D_MODEL = 2048
PAGE_SIZE = 128
ATT_HEADS = 8
ATT_HD = 128
ATT_WIDTH = ATT_HEADS * ATT_HD
MOBA_BLOCK = 256
MOBA_TOPK = 3
GLA_HEADS = 4
GLA_DK = 128
GLA_DV = 256
GLA_KW = GLA_HEADS * GLA_DK
GLA_VW = GLA_HEADS * GLA_DV
GLA_GATE_RANK = 16
GLA_TAU = 16.0
GLA_CHUNK = 64
EPS = 1e-6

LANE = 128
VMEM_LIMIT = 48 * 1024 * 1024
PAGES_PER_STEP = 16

F32 = jnp.float32
BF16 = jnp.bfloat16
HIGHEST = lax.Precision.HIGHEST
NEG_INF = float("-inf")


def _params(*sem):
    return pltpu.CompilerParams(dimension_semantics=sem, vmem_limit_bytes=VMEM_LIMIT)


def _dot(a, b, precision=None):
    return jnp.dot(a, b, preferred_element_type=F32, precision=precision)


def _dot_nt(a, b, precision=None):
    return lax.dot_general(a, b, (((1,), (1,)), ((), ())),
                           preferred_element_type=F32, precision=precision)


def _dot_tn(a, b, precision=None):
    return lax.dot_general(a, b, (((0,), (0,)), ((), ())),
                           preferred_element_type=F32, precision=precision)


def _rms(x, g):
    return x * lax.rsqrt(jnp.mean(x * x, axis=-1, keepdims=True) + EPS) * g


def _silu(x):
    return x * (1.0 / (1.0 + jnp.exp(-x)))


def _log_sigmoid(x):
    return jnp.minimum(x, 0.0) - jnp.log1p(jnp.exp(-jnp.abs(x)))


def _norm_kernel(x_ref, g_ref, o_ref):
    o_ref[...] = _rms(x_ref[...], g_ref[...]).astype(o_ref.dtype)


def rms_norm_bf16(x, g, tm):
    m, d = x.shape
    return pl.pallas_call(
        _norm_kernel,
        grid=(m // tm,),
        in_specs=[pl.BlockSpec((tm, d), lambda i: (i, 0)),
                  pl.BlockSpec((1, d), lambda i: (0, 0))],
        out_specs=pl.BlockSpec((tm, d), lambda i: (i, 0)),
        out_shape=jax.ShapeDtypeStruct((m, d), BF16),
        compiler_params=_params("parallel"),
        name="rms_norm",
    )(x, g.reshape(1, d))


def _proj_kernel(x_ref, w_ref, o_ref):
    o_ref[...] = _dot(x_ref[...], w_ref[...])


def project(xn, w, col_block, tn, tm):
    m, k = xn.shape
    return pl.pallas_call(
        _proj_kernel,
        grid=(m // tm,),
        in_specs=[pl.BlockSpec((tm, k), lambda i: (i, 0)),
                  pl.BlockSpec((k, tn), lambda i: (0, col_block))],
        out_specs=pl.BlockSpec((tm, tn), lambda i: (i, 0)),
        out_shape=jax.ShapeDtypeStruct((m, tn), F32),
        compiler_params=_params("parallel"),
        name="in_proj",
    )(xn, w)


def _out_proj_kernel(x_ref, att_ref, gla_ref, r_ref, ag_ref, gg_ref, w_ref, o_ref):
    parts = [_rms(att_ref[...], ag_ref[...]).astype(BF16)]
    for h in range(GLA_HEADS):
        cols = slice(h * GLA_DV, (h + 1) * GLA_DV)
        y = _rms(gla_ref[:, cols], gg_ref[...]) * _silu(r_ref[:, cols])
        parts.append(y.astype(BF16))
    y = jnp.concatenate(parts, axis=-1)
    o_ref[...] = x_ref[...] + _dot(y, w_ref[...])


def out_project(x, att, gla_o, r_g, att_g, gla_g, w_o, tm):
    m, d = x.shape
    row = lambda i: (i, 0)
    fixed = lambda i: (0, 0)
    return pl.pallas_call(
        _out_proj_kernel,
        grid=(m // tm,),
        in_specs=[pl.BlockSpec((tm, d), row),
                  pl.BlockSpec((tm, ATT_WIDTH), row),
                  pl.BlockSpec((tm, GLA_VW), row),
                  pl.BlockSpec((tm, GLA_VW), row),
                  pl.BlockSpec((1, ATT_WIDTH), fixed),
                  pl.BlockSpec((1, GLA_DV), fixed),
                  pl.BlockSpec(w_o.shape, fixed)],
        out_specs=pl.BlockSpec((tm, d), row),
        out_shape=jax.ShapeDtypeStruct((m, d), F32),
        compiler_params=_params("parallel"),
        name="out_proj",
    )(x, att, gla_o, r_g, att_g.reshape(1, -1), gla_g.reshape(1, -1), w_o)


def _ffn_kernel(h_ref, g_ref, wg_ref, wu_ref, wd_ref, fg_ref, o_ref, hn_scr, acc_scr):
    f = pl.program_id(1)

    @pl.when(f == 0)
    def _():
        hn_scr[...] = _rms(h_ref[...], g_ref[...]).astype(BF16)
        acc_scr[...] = jnp.zeros_like(acc_scr)

    hn = hn_scr[...]
    act = _silu(_dot(hn, wg_ref[...])) * _dot(hn, wu_ref[...])
    acc_scr[...] += _dot(act.astype(BF16), wd_ref[...])

    @pl.when(f == pl.num_programs(1) - 1)
    def _():
        o_ref[...] = _rms(h_ref[...] + acc_scr[...], fg_ref[...])


def ffn_final_norm(h, ffn_g, w_gate, w_up, w_down, final_g, tm, tf):
    m, d = h.shape
    d_ff = w_gate.shape[1]
    return pl.pallas_call(
        _ffn_kernel,
        grid=(m // tm, d_ff // tf),
        in_specs=[pl.BlockSpec((tm, d), lambda i, f: (i, 0)),
                  pl.BlockSpec((1, d), lambda i, f: (0, 0)),
                  pl.BlockSpec((d, tf), lambda i, f: (0, f)),
                  pl.BlockSpec((d, tf), lambda i, f: (0, f)),
                  pl.BlockSpec((tf, d), lambda i, f: (f, 0)),
                  pl.BlockSpec((1, d), lambda i, f: (0, 0))],
        out_specs=pl.BlockSpec((tm, d), lambda i, f: (i, 0)),
        out_shape=jax.ShapeDtypeStruct((m, d), F32),
        scratch_shapes=[pltpu.VMEM((tm, d), BF16), pltpu.VMEM((tm, d), F32)],
        compiler_params=_params("parallel", "arbitrary"),
        name="ffn",
    )(h, ffn_g.reshape(1, d), w_gate, w_up, w_down, final_g.reshape(1, d))


def _moba_prompt_kernel(slopes_ref, q_ref, k_ref, v_ref, o_ref):
    seq = q_ref.shape[0]
    nb = seq // MOBA_BLOCK
    slope = slopes_ref[pl.program_id(1)]
    k = k_ref[...]
    kb = k.astype(BF16)
    vb = v_ref[...].astype(BF16)
    means = [jnp.sum(k[n * MOBA_BLOCK:(n + 1) * MOBA_BLOCK], axis=0, keepdims=True) / MOBA_BLOCK
             for n in range(nb)]
    means = jnp.concatenate(means, axis=0)
    blk_id = lax.broadcasted_iota(jnp.int32, (nb, MOBA_BLOCK), 0)
    r_loc = lax.broadcasted_iota(jnp.int32, (MOBA_BLOCK, MOBA_BLOCK), 0)
    c_loc = lax.broadcasted_iota(jnp.int32, (MOBA_BLOCK, MOBA_BLOCK), 1)
    rel = (r_loc - c_loc).astype(F32)
    for i in range(nb):
        q = q_ref[i * MOBA_BLOCK:(i + 1) * MOBA_BLOCK, :] * (ATT_HD ** -0.5)
        kv_len = (i + 1) * MOBA_BLOCK
        s = _dot_nt(q.astype(BF16), kb[:kv_len])
        sel = None
        if i > MOBA_TOPK:
            gate_t = _dot_nt(means, q, precision=HIGHEST)
            rank = jnp.zeros((nb, MOBA_BLOCK), jnp.int32)
            for m in range(i):
                gm = gate_t[m:m + 1, :]
                beats = (gm > gate_t) | ((gm == gate_t) & (blk_id > m))
                rank = rank + beats.astype(jnp.int32)
            sel_t = jnp.where((rank < MOBA_TOPK) & (blk_id < i), 1.0, 0.0)
            sel_t = jnp.concatenate([sel_t, jnp.zeros((LANE - nb, MOBA_BLOCK), F32)], axis=0)
            sel = jnp.transpose(sel_t)
        pieces = []
        for n in range(i + 1):
            s_n = s[:, n * MOBA_BLOCK:(n + 1) * MOBA_BLOCK]
            s_n = s_n - slope * (rel + float((i - n) * MOBA_BLOCK))
            if n == i:
                s_n = jnp.where(c_loc <= r_loc, s_n, NEG_INF)
            elif sel is not None:
                s_n = jnp.where(sel[:, n:n + 1] > 0.5, s_n, NEG_INF)
            pieces.append(s_n)
        s = jnp.concatenate(pieces, axis=-1)
        m_row = jnp.max(s, axis=-1, keepdims=True)
        p = jnp.exp(s - m_row)
        l_row = jnp.sum(p, axis=-1, keepdims=True)
        o = _dot(p.astype(BF16), vb[:kv_len])
        o_ref[i * MOBA_BLOCK:(i + 1) * MOBA_BLOCK, :] = o / l_row


def moba_prompt(q, k, v, slopes, batch, seq):
    spec = pl.BlockSpec((seq, ATT_HD), lambda b, h: (b, h))
    return pl.pallas_call(
        _moba_prompt_kernel,
        grid=(batch, ATT_HEADS),
        in_specs=[pl.BlockSpec(memory_space=pltpu.SMEM), spec, spec, spec],
        out_specs=spec,
        out_shape=jax.ShapeDtypeStruct(q.shape, F32),
        compiler_params=_params("parallel", "parallel"),
        name="moba_prompt",
    )(slopes, q, k, v)


def _gla_prompt_kernel(q_ref, k_ref, v_ref, glow_ref, wgu_ref, bg_ref, o_ref, s_ref,
                       g_scr, st_scr):
    seq = q_ref.shape[0]
    c = GLA_CHUNK
    x = _dot(glow_ref[...], wgu_ref[...], precision=HIGHEST) + bg_ref[...]
    g_scr[...] = _log_sigmoid(x) / GLA_TAU
    st_scr[...] = jnp.zeros_like(st_scr)
    r_i = lax.broadcasted_iota(jnp.int32, (c, c), 0)
    c_i = lax.broadcasted_iota(jnp.int32, (c, c), 1)
    causal = c_i <= r_i
    tril = jnp.where(causal, 1.0, 0.0).astype(F32)

    def chunk(ci, carry):
        r0 = pl.multiple_of(ci * c, c)
        q = q_ref[pl.ds(r0, c), :] * (GLA_DK ** -0.5)
        k = k_ref[pl.ds(r0, c), :]
        v = v_ref[pl.ds(r0, c), :].astype(BF16)
        b = _dot(tril, g_scr[pl.ds(r0, c), :], precision=HIGHEST)
        b_mid = b[c // 2:c // 2 + 1, :]
        b_last = b[c - 1:c, :]
        q_in = (q * jnp.exp(b)).astype(BF16)
        q_mid = (q * jnp.exp(b - b_mid)).astype(BF16)
        k_mid = (k * jnp.exp(b_mid - b)).astype(BF16)
        k_out = (k * jnp.exp(b_last - b)).astype(BF16)
        att = jnp.where(causal, _dot_nt(q_mid, k_mid), 0.0)
        st = st_scr[...]
        o_ref[pl.ds(r0, c), :] = _dot_nt(q_in, st.astype(BF16)) + _dot(att.astype(BF16), v)
        st_scr[...] = st * jnp.exp(b_last) + _dot_tn(v, k_out)
        return carry

    lax.fori_loop(0, seq // c, chunk, 0)
    s_ref[0, 0] = jnp.transpose(st_scr[...])


def gla_prompt(qk, v, glow, wgu, bg, batch, seq):
    return pl.pallas_call(
        _gla_prompt_kernel,
        grid=(batch, GLA_HEADS),
        in_specs=[pl.BlockSpec((seq, GLA_DK), lambda b, h: (b, h)),
                  pl.BlockSpec((seq, GLA_DK), lambda b, h: (b, GLA_HEADS + h)),
                  pl.BlockSpec((seq, GLA_DV), lambda b, h: (b, h)),
                  pl.BlockSpec((seq, LANE), lambda b, h: (b, 0)),
                  pl.BlockSpec((LANE, GLA_DK), lambda b, h: (0, h)),
                  pl.BlockSpec((1, GLA_DK), lambda b, h: (0, h))],
        out_specs=[pl.BlockSpec((seq, GLA_DV), lambda b, h: (b, h)),
                   pl.BlockSpec((1, 1, GLA_DK, GLA_DV), lambda b, h: (b, h, 0, 0))],
        out_shape=[jax.ShapeDtypeStruct((batch * seq, GLA_VW), F32),
                   jax.ShapeDtypeStruct((batch, GLA_HEADS, GLA_DK, GLA_DV), F32)],
        scratch_shapes=[pltpu.VMEM((seq, GLA_DK), F32), pltpu.VMEM((GLA_DV, GLA_DK), F32)],
        compiler_params=_params("parallel", "parallel"),
        name="gla_prompt",
    )(qk, qk, v, glow, wgu, bg)


def _page_sum_kernel(pt_ref, *refs):
    pages, o_ref = refs[:-1], refs[-1]
    per_block = MOBA_BLOCK // PAGE_SIZE
    for j in range(len(pages) // per_block):
        tot = jnp.sum(pages[per_block * j][...], axis=0, keepdims=True)
        for r in range(1, per_block):
            tot = tot + jnp.sum(pages[per_block * j + r][...], axis=0, keepdims=True)
        o_ref[0, j:j + 1, :] = tot


def page_block_sums(pool_k, page_table_flat, n_seq, n_pages):
    per_block = MOBA_BLOCK // PAGE_SIZE
    steps = n_pages // PAGES_PER_STEP
    blocks_per_step = PAGES_PER_STEP // per_block

    def page_map(r, b, s, pt):
        return (pt[b * n_pages + s * PAGES_PER_STEP + r], 0, 0)

    in_specs = [pl.BlockSpec((None, PAGE_SIZE, ATT_WIDTH), functools.partial(page_map, r))
                for r in range(PAGES_PER_STEP)]
    return pl.pallas_call(
        _page_sum_kernel,
        grid_spec=pltpu.PrefetchScalarGridSpec(
            num_scalar_prefetch=1,
            grid=(n_seq, steps),
            in_specs=in_specs,
            out_specs=pl.BlockSpec((1, blocks_per_step, ATT_WIDTH), lambda b, s, pt: (b, s, 0)),
        ),
        out_shape=jax.ShapeDtypeStruct((n_seq, n_pages // per_block, ATT_WIDTH), F32),
        compiler_params=_params("parallel", "arbitrary"),
        name="page_sums",
    )(page_table_flat, *([pool_k] * PAGES_PER_STEP))


def _block_select_kernel(sums_ref, q_ref, idx_ref):
    nb = sums_ref.shape[1]
    prod = sums_ref[0] * (q_ref[0] * (ATT_HD ** -0.5))
    head = lax.broadcasted_iota(jnp.int32, (nb, ATT_HEADS), 1)
    gate = jnp.zeros((nb, ATT_HEADS), F32)
    for h in range(ATT_HEADS):
        g_h = jnp.sum(prod[:, h * ATT_HD:(h + 1) * ATT_HD], axis=-1, keepdims=True) / MOBA_BLOCK
        gate = jnp.where(head == h, g_h, gate)
    blk = lax.broadcasted_iota(jnp.int32, (nb, ATT_HEADS), 0)
    for j in range(MOBA_TOPK):
        best = jnp.max(gate, axis=0, keepdims=True)
        idx = jnp.min(jnp.where(gate == best, blk, nb), axis=0, keepdims=True)
        idx_ref[0, j:j + 1, :] = idx
        gate = jnp.where(blk == idx, NEG_INF, gate)


def block_select(sums, q_s):
    n_seq, nb, _ = sums.shape
    return pl.pallas_call(
        _block_select_kernel,
        grid=(n_seq,),
        in_specs=[pl.BlockSpec((1, nb, ATT_WIDTH), lambda b: (b, 0, 0)),
                  pl.BlockSpec((1, 1, ATT_WIDTH), lambda b: (b, 0, 0))],
        out_specs=pl.BlockSpec((1, MOBA_TOPK, ATT_HEADS), lambda b: (b, 0, 0)),
        out_shape=jax.ShapeDtypeStruct((n_seq, MOBA_TOPK, ATT_HEADS), jnp.int32),
        compiler_params=_params("parallel"),
        name="block_select",
    )(sums, q_s)


def _moba_sample_kernel(sel_ref, pt_ref, slopes_ref, q_ref, kn_ref, vn_ref, *refs, past):
    n_pg = MOBA_TOPK * (MOBA_BLOCK // PAGE_SIZE)
    k_pages, v_pages, o_ref = refs[:n_pg], refs[n_pg:2 * n_pg], refs[-1]
    b, h = pl.program_id(0), pl.program_id(1)
    slope = slopes_ref[h]
    q = q_ref[0] * (ATT_HD ** -0.5)
    q8 = jnp.broadcast_to(q, (8, ATT_HD))
    lane = lax.broadcasted_iota(jnp.int32, (1, PAGE_SIZE), 1)
    scores = []
    for p in range(n_pg):
        j, half = divmod(p, MOBA_BLOCK // PAGE_SIZE)
        blk = sel_ref[(b * MOBA_TOPK + j) * ATT_HEADS + h]
        pos0 = blk * MOBA_BLOCK + half * PAGE_SIZE
        dist = (past - pos0 - lane).astype(F32)
        s_p = _dot_nt(q8, k_pages[p][...], precision=HIGHEST)[0:1, :]
        scores.append(s_p - slope * dist)
    s_sel = jnp.concatenate(scores, axis=-1)
    s_own = jnp.sum(q * kn_ref[0], axis=-1, keepdims=True)
    m = jnp.maximum(jnp.max(s_sel, axis=-1, keepdims=True), s_own)
    p_sel = jnp.exp(s_sel - m)
    p_own = jnp.exp(s_own - m)
    denom = jnp.sum(p_sel, axis=-1, keepdims=True) + p_own
    acc = p_own * vn_ref[0]
    for p in range(n_pg):
        p8 = jnp.broadcast_to(p_sel[:, p * PAGE_SIZE:(p + 1) * PAGE_SIZE], (8, PAGE_SIZE))
        acc = acc + _dot(p8, v_pages[p][...], precision=HIGHEST)[0:1, :]
    o_ref[0] = acc / denom


def moba_sample(q_s, k_new, v_new, pool_k, pool_v, sel_flat, page_table_flat, slopes, n_pages):
    n_seq = q_s.shape[0]
    per_block = MOBA_BLOCK // PAGE_SIZE

    def page_map(j, half, b, h, sel, pt):
        blk = sel[(b * MOBA_TOPK + j) * ATT_HEADS + h]
        return (pt[b * n_pages + blk * per_block + half], 0, h)

    page_specs = [pl.BlockSpec((None, PAGE_SIZE, ATT_HD), functools.partial(page_map, j, half))
                  for j in range(MOBA_TOPK) for half in range(per_block)]
    tok = pl.BlockSpec((1, 1, ATT_HD), lambda b, h, sel, pt: (b, 0, h))
    n_pg = len(page_specs)
    return pl.pallas_call(
        functools.partial(_moba_sample_kernel, past=n_pages * PAGE_SIZE),
        grid_spec=pltpu.PrefetchScalarGridSpec(
            num_scalar_prefetch=2,
            grid=(n_seq, ATT_HEADS),
            in_specs=[pl.BlockSpec(memory_space=pltpu.SMEM), tok, tok, tok] + page_specs * 2,
            out_specs=tok,
        ),
        out_shape=jax.ShapeDtypeStruct(q_s.shape, F32),
        compiler_params=_params("parallel", "parallel"),
        name="moba_sample",
    )(sel_flat, page_table_flat, slopes, q_s, k_new, v_new, *([pool_k] * n_pg), *([pool_v] * n_pg))


def _to_column(row, eye):
    return jnp.sum(jnp.where(eye, row, 0.0), axis=-1, keepdims=True)


def _gla_sample_kernel(q_ref, k_ref, v_ref, glow_ref, wgu_ref, bg_ref, s0_ref, o_ref, s_ref):
    eye = (lax.broadcasted_iota(jnp.int32, (GLA_DK, GLA_DK), 0)
           == lax.broadcasted_iota(jnp.int32, (GLA_DK, GLA_DK), 1))
    glow8 = jnp.broadcast_to(glow_ref[0], (8, LANE))
    x = _dot(glow8, wgu_ref[...], precision=HIGHEST)[0:1, :] + bg_ref[...]
    a = jnp.exp(_log_sigmoid(x) / GLA_TAU)
    for h in range(GLA_HEADS):
        kc = slice(h * GLA_DK, (h + 1) * GLA_DK)
        vc = slice(h * GLA_DV, (h + 1) * GLA_DV)
        a_col = _to_column(a[:, kc], eye)
        k_col = _to_column(k_ref[0][:, kc], eye)
        q_col = _to_column(q_ref[0][:, kc] * (GLA_DK ** -0.5), eye)
        s_new = a_col * s0_ref[0, h] + k_col * v_ref[0][:, vc]
        s_ref[0, h] = s_new
        o_ref[0, :, vc] = jnp.sum(q_col * s_new, axis=0, keepdims=True)


def gla_sample(qk, v, glow, wgu, bg, s0):
    n_seq = qk.shape[0]
    st_spec = pl.BlockSpec((1, GLA_HEADS, GLA_DK, GLA_DV), lambda b: (b, 0, 0, 0))
    return pl.pallas_call(
        _gla_sample_kernel,
        grid=(n_seq,),
        in_specs=[pl.BlockSpec((1, 1, GLA_KW), lambda b: (b, 0, 0)),
                  pl.BlockSpec((1, 1, GLA_KW), lambda b: (b, 0, 1)),
                  pl.BlockSpec((1, 1, GLA_VW), lambda b: (b, 0, 0)),
                  pl.BlockSpec((1, 1, LANE), lambda b: (b, 0, 0)),
                  pl.BlockSpec((LANE, GLA_KW), lambda b: (0, 0)),
                  pl.BlockSpec((1, GLA_KW), lambda b: (0, 0)),
                  st_spec],
        out_specs=[pl.BlockSpec((1, 1, GLA_VW), lambda b: (b, 0, 0)), st_spec],
        out_shape=[jax.ShapeDtypeStruct((n_seq, 1, GLA_VW), F32),
                   jax.ShapeDtypeStruct(s0.shape, F32)],
        compiler_params=_params("parallel"),
        name="gla_sample",
    )(qk, qk, v, glow, wgu, bg, s0)


def _input_projections(x, norm_g, w_main, w_glow, tm):
    xn = rms_norm_bf16(x, norm_g, tm)
    outs = [project(xn, w_main, c, ATT_WIDTH, tm) for c in range(w_main.shape[1] // ATT_WIDTH)]
    outs.append(project(xn, w_glow, 0, LANE, tm))
    return outs


def kernel(x_prompt, x_sample, cache_k, cache_v, state_gla, page_table, attn_norm_g, w_in,
           w_gate_up, b_gate, att_out_norm_g, gla_norm_g, w_o, ffn_norm_g, w_ffn_gate,
           w_ffn_up, w_ffn_down, final_norm_g):
    depth = w_in.shape[0]
    assert depth == 1, "single-layer trunk"
    batch, seq, d = x_prompt.shape
    n_seq, dec_seq, _ = x_sample.shape
    assert dec_seq == 1 and seq % MOBA_BLOCK == 0 and seq % GLA_CHUNK == 0
    n_pages = page_table.shape[1]
    assert (n_pages * PAGE_SIZE) % MOBA_BLOCK == 0 and n_pages % PAGES_PER_STEP == 0
    n_pool = cache_k.shape[1]

    slopes = jnp.exp2(-8.0 * jnp.arange(1, ATT_HEADS + 1, dtype=F32) / ATT_HEADS)
    main_w = 3 * ATT_WIDTH + 2 * GLA_KW + 2 * GLA_VW
    w_in_b = w_in[0].astype(BF16)
    w_main = w_in_b[:, :main_w]
    w_glow = jnp.pad(w_in_b[:, main_w:], ((0, 0), (0, LANE - GLA_GATE_RANK)))
    wgu = jnp.pad(w_gate_up[0], ((0, LANE - GLA_GATE_RANK), (0, 0)))
    bg = b_gate[0].reshape(1, GLA_KW)
    w_o_b = w_o[0].astype(BF16)
    w_fg = w_ffn_gate[0].astype(BF16)
    w_fu = w_ffn_up[0].astype(BF16)
    w_fd = w_ffn_down[0].astype(BF16)

    xp = x_prompt.reshape(batch * seq, d)
    q_a, k_a, v_a, qk_g, v_g, r_g, glow = _input_projections(xp, attn_norm_g[0], w_main, w_glow, 512)
    att = moba_prompt(q_a, k_a, v_a, slopes, batch, seq)
    gla_o, st_p = gla_prompt(qk_g, v_g, glow, wgu, bg, batch, seq)
    h_p = out_project(xp, att, gla_o, r_g, att_out_norm_g[0], gla_norm_g[0], w_o_b, 256)
    y_p = ffn_final_norm(h_p, ffn_norm_g[0], w_fg, w_fu, w_fd, final_norm_g, 512, 512)

    xs = x_sample.reshape(n_seq, d)
    q_s, k_s, v_s, qk_s, vg_s, r_s, glow_s = _input_projections(xs, attn_norm_g[0], w_main, w_glow, n_seq)
    pool_k = cache_k[0].reshape(n_pool, PAGE_SIZE, ATT_WIDTH)
    pool_v = cache_v[0].reshape(n_pool, PAGE_SIZE, ATT_WIDTH)
    pt_flat = page_table.reshape(-1)
    tok3 = lambda a: a.reshape(n_seq, 1, a.shape[-1])
    sums = page_block_sums(pool_k, pt_flat, n_seq, n_pages)
    sel = block_select(sums, tok3(q_s))
    att_s = moba_sample(tok3(q_s), tok3(k_s), tok3(v_s), pool_k, pool_v, sel.reshape(-1), pt_flat,
                        slopes, n_pages)
    gla_s, st_s = gla_sample(tok3(qk_s), tok3(vg_s), tok3(glow_s), wgu, bg, state_gla[0])
    h_s = out_project(xs, att_s.reshape(n_seq, ATT_WIDTH), gla_s.reshape(n_seq, GLA_VW), r_s,
                      att_out_norm_g[0], gla_norm_g[0], w_o_b, n_seq)
    y_s = ffn_final_norm(h_s, ffn_norm_g[0], w_fg, w_fu, w_fd, final_norm_g, n_seq, 512)

    kv_p = (depth, batch, seq, ATT_HEADS, ATT_HD)
    kv_s = (depth, n_seq, dec_seq, ATT_HEADS, ATT_HD)
    return (y_p.reshape(batch, seq, d), y_s.reshape(n_seq, dec_seq, d),
            k_a.reshape(kv_p), v_a.reshape(kv_p), k_s.reshape(kv_s), v_s.reshape(kv_s),
            st_p[None], st_s[None])
```

```python
import functools

import jax
import jax.numpy as jnp
from jax import lax
from jax.experimental import pallas as pl
from jax.experimental.pallas import tpu as pltpu

D_MODEL = 2048
PAGE_SIZE = 128
ATT_HEADS = 8
ATT_HD = 128
ATT_WIDTH = ATT_HEADS * ATT_HD
MOBA_BLOCK = 256
MOBA_TOPK = 3
GLA_HEADS = 4
GLA_DK = 128
GLA_DV = 256
GLA_KW = GLA_HEADS * GLA_DK
GLA_VW = GLA_HEADS * GLA_DV
GLA_GATE_RANK = 16
GLA_TAU = 16.0
GLA_CHUNK = 64
EPS = 1e-6

LANE = 128
SUBLANE = 8
VMEM_LIMIT = 48 * 1024 * 1024
PAGES_PER_STEP = 16
GLA_GROUP = 8

F32 = jnp.float32
BF16 = jnp.bfloat16
NEG_INF = float("-inf")
LOG2E = 1.4426950408889634
MASKED = -1e30


def _params(*sem):
    return pltpu.CompilerParams(dimension_semantics=sem, vmem_limit_bytes=VMEM_LIMIT)


def _dot(a, b, precision=None):
    return jnp.dot(a, b, preferred_element_type=F32, precision=precision)


def _dot_nt(a, b, precision=None):
    return lax.dot_general(a, b, (((1,), (1,)), ((), ())),
                           preferred_element_type=F32, precision=precision)


def _dot_tn(a, b, precision=None):
    return lax.dot_general(a, b, (((0,), (0,)), ((), ())),
                           preferred_element_type=F32, precision=precision)


def _split2(x):
    hi = x.astype(BF16)
    return hi, (x - hi.astype(F32)).astype(BF16)


def _split3(x):
    hi = x.astype(BF16)
    r = x - hi.astype(F32)
    mid = r.astype(BF16)
    return hi, mid, (r - mid.astype(F32)).astype(BF16)


def _dot_x3(a, b):
    a_hi, a_lo = _split2(a)
    b_hi, b_lo = _split2(b)
    return _dot(a_hi, b_hi) + (_dot(a_hi, b_lo) + _dot(a_lo, b_hi))


def _rms(x, g):
    return x * lax.rsqrt(jnp.mean(x * x, axis=-1, keepdims=True) + EPS) * g


def _silu(x):
    return x * (1.0 / (1.0 + jnp.exp(-x)))


def _log_sigmoid(x):
    return jnp.minimum(x, 0.0) - jnp.log1p(jnp.exp(-jnp.abs(x)))


def _norm_kernel(x_ref, g_ref, o_ref):
    o_ref[...] = _rms(x_ref[...], g_ref[...]).astype(o_ref.dtype)


def rms_norm_bf16(x, g, tm):
    m, d = x.shape
    return pl.pallas_call(
        _norm_kernel,
        grid=(m // tm,),
        in_specs=[pl.BlockSpec((tm, d), lambda i: (i, 0)),
                  pl.BlockSpec((1, d), lambda i: (0, 0))],
        out_specs=pl.BlockSpec((tm, d), lambda i: (i, 0)),
        out_shape=jax.ShapeDtypeStruct((m, d), BF16),
        compiler_params=_params("parallel"),
        name="rms_norm",
    )(x, g.reshape(1, d))


def _proj_kernel(x_ref, w_ref, o_ref):
    o_ref[...] = _dot(x_ref[...], w_ref[...])


def project(xn, w, col_block, tn, tm):
    m, k = xn.shape
    return pl.pallas_call(
        _proj_kernel,
        grid=(m // tm,),
        in_specs=[pl.BlockSpec((tm, k), lambda i: (i, 0)),
                  pl.BlockSpec((k, tn), lambda i: (0, col_block))],
        out_specs=pl.BlockSpec((tm, tn), lambda i: (i, 0)),
        out_shape=jax.ShapeDtypeStruct((m, tn), F32),
        compiler_params=_params("parallel"),
        name="in_proj",
    )(xn, w)


def _out_proj_kernel(x_ref, att_ref, gla_ref, r_ref, ag_ref, gg_ref, w_ref, o_ref):
    parts = [_rms(att_ref[...], ag_ref[...]).astype(BF16)]
    for h in range(GLA_HEADS):
        cols = slice(h * GLA_DV, (h + 1) * GLA_DV)
        y = _rms(gla_ref[:, cols], gg_ref[...]) * _silu(r_ref[:, cols])
        parts.append(y.astype(BF16))
    y = jnp.concatenate(parts, axis=-1)
    o_ref[...] = x_ref[...] + _dot(y, w_ref[...])


def out_project(x, att, gla_o, r_g, att_g, gla_g, w_o, tm):
    m, d = x.shape
    row = lambda i: (i, 0)
    fixed = lambda i: (0, 0)
    return pl.pallas_call(
        _out_proj_kernel,
        grid=(m // tm,),
        in_specs=[pl.BlockSpec((tm, d), row),
                  pl.BlockSpec((tm, ATT_WIDTH), row),
                  pl.BlockSpec((tm, GLA_VW), row),
                  pl.BlockSpec((tm, GLA_VW), row),
                  pl.BlockSpec((1, ATT_WIDTH), fixed),
                  pl.BlockSpec((1, GLA_DV), fixed),
                  pl.BlockSpec(w_o.shape, fixed)],
        out_specs=pl.BlockSpec((tm, d), row),
        out_shape=jax.ShapeDtypeStruct((m, d), F32),
        compiler_params=_params("parallel"),
        name="out_proj",
    )(x, att, gla_o, r_g, att_g.reshape(1, -1), gla_g.reshape(1, -1), w_o)


def _ffn_kernel(h_ref, g_ref, wg_ref, wu_ref, wd_ref, fg_ref, o_ref, hn_scr, acc_scr):
    f = pl.program_id(1)

    @pl.when(f == 0)
    def _():
        hn_scr[...] = _rms(h_ref[...], g_ref[...]).astype(BF16)
        acc_scr[...] = jnp.zeros_like(acc_scr)

    hn = hn_scr[...]
    act = _silu(_dot(hn, wg_ref[...])) * _dot(hn, wu_ref[...])
    acc_scr[...] += _dot(act.astype(BF16), wd_ref[...])

    @pl.when(f == pl.num_programs(1) - 1)
    def _():
        o_ref[...] = _rms(h_ref[...] + acc_scr[...], fg_ref[...])


def ffn_final_norm(h, ffn_g, w_gate, w_up, w_down, final_g, tm, tf):
    m, d = h.shape
    d_ff = w_gate.shape[1]
    return pl.pallas_call(
        _ffn_kernel,
        grid=(m // tm, d_ff // tf),
        in_specs=[pl.BlockSpec((tm, d), lambda i, f: (i, 0)),
                  pl.BlockSpec((1, d), lambda i, f: (0, 0)),
                  pl.BlockSpec((d, tf), lambda i, f: (0, f)),
                  pl.BlockSpec((d, tf), lambda i, f: (0, f)),
                  pl.BlockSpec((tf, d), lambda i, f: (f, 0)),
                  pl.BlockSpec((1, d), lambda i, f: (0, 0))],
        out_specs=pl.BlockSpec((tm, d), lambda i, f: (i, 0)),
        out_shape=jax.ShapeDtypeStruct((m, d), F32),
        scratch_shapes=[pltpu.VMEM((tm, d), BF16), pltpu.VMEM((tm, d), F32)],
        compiler_params=_params("parallel", "arbitrary"),
        name="ffn",
    )(h, ffn_g.reshape(1, d), w_gate, w_up, w_down, final_g.reshape(1, d))


def _moba_prompt_kernel(slopes_ref, q_ref, k_ref, v_ref, o_ref):
    seq = q_ref.shape[0]
    nb = seq // MOBA_BLOCK
    slope2 = slopes_ref[pl.program_id(1)] * LOG2E
    k = k_ref[...]
    means = [jnp.sum(k[n * MOBA_BLOCK:(n + 1) * MOBA_BLOCK], axis=0, keepdims=True) / MOBA_BLOCK
             for n in range(nb)]
    mean_hi, mean_lo = _split2(jnp.concatenate(means, axis=0))
    row = lax.broadcasted_iota(jnp.int32, (seq, LANE), 0)
    lane = lax.broadcasted_iota(jnp.int32, (seq, LANE), 1)
    b_hi, b_mid, b_lo = _split3(slope2 * row.astype(F32))
    ext = jnp.where(lane == lax.shift_right_logical(row, MOBA_BLOCK.bit_length() - 1), 1.0, 0.0)
    ext = ext.astype(BF16)
    ext = jnp.where(lane == nb, b_hi, ext)
    ext = jnp.where(lane == nb + 1, b_mid, ext)
    ext = jnp.where(lane == nb + 2, b_lo, ext)
    k_ext = jnp.concatenate([k.astype(BF16), ext], axis=-1)
    v_t = jnp.transpose(v_ref[...]).astype(BF16)
    blk_id = lax.broadcasted_iota(jnp.int32, (nb, MOBA_BLOCK), 0)
    ext_row = lax.broadcasted_iota(jnp.int32, (LANE, MOBA_BLOCK), 0)
    ones_rows = jnp.where((ext_row >= nb) & (ext_row < nb + 3), 1.0, 0.0)
    key_loc = lax.broadcasted_iota(jnp.int32, (MOBA_BLOCK, MOBA_BLOCK), 0)
    qry_loc = lax.broadcasted_iota(jnp.int32, (MOBA_BLOCK, MOBA_BLOCK), 1)

    def scores(i):
        q = q_ref[i * MOBA_BLOCK:(i + 1) * MOBA_BLOCK, :] * (ATT_HD ** -0.5 * LOG2E)
        q_t = jnp.transpose(q)
        q_ext = ones_rows
        if i > MOBA_TOPK:
            qt_hi, qt_lo = _split2(q_t)
            gate_t = _dot(mean_hi, qt_hi) + (_dot(mean_hi, qt_lo) + _dot(mean_lo, qt_hi))
            rank = jnp.zeros((nb, MOBA_BLOCK), jnp.int32)
            for m in range(i):
                gm = gate_t[m:m + 1, :]
                beats = (gm > gate_t) | ((gm == gate_t) & (blk_id > m))
                rank = rank + beats.astype(jnp.int32)
            bias_t = jnp.where((rank < MOBA_TOPK) | (blk_id >= i), 0.0, MASKED)
            bias_t = jnp.concatenate([bias_t, jnp.zeros((LANE - nb, MOBA_BLOCK), F32)], axis=0)
            q_ext = bias_t + ones_rows
        q_full = jnp.concatenate([q_t.astype(BF16), q_ext.astype(BF16)], axis=0)
        return _dot(k_ext[:(i + 1) * MOBA_BLOCK], q_full)

    s_next = scores(0)
    for i in range(nb):
        s_t = s_next
        if i + 1 < nb:
            s_next = scores(i + 1)
        own = jnp.where(key_loc <= qry_loc, s_t[i * MOBA_BLOCK:], NEG_INF)
        s_t = own if i == 0 else jnp.concatenate([s_t[:i * MOBA_BLOCK], own], axis=0)
        m_row = jnp.max(s_t, axis=0, keepdims=True)
        p = jnp.exp2(s_t - m_row)
        l_row = jnp.sum(p, axis=0, keepdims=True)
        o_t = _dot(v_t[:, :(i + 1) * MOBA_BLOCK], p.astype(BF16))
        o_ref[i * MOBA_BLOCK:(i + 1) * MOBA_BLOCK, :] = jnp.transpose(o_t * (1.0 / l_row))


def moba_prompt(q, k, v, slopes, batch, seq):
    spec = pl.BlockSpec((seq, ATT_HD), lambda b, h: (b, h))
    return pl.pallas_call(
        _moba_prompt_kernel,
        grid=(batch, ATT_HEADS),
        in_specs=[pl.BlockSpec(memory_space=pltpu.SMEM), spec, spec, spec],
        out_specs=spec,
        out_shape=jax.ShapeDtypeStruct(q.shape, F32),
        compiler_params=_params("parallel", "parallel"),
        name="moba_prompt",
    )(slopes, q, k, v)


def _gla_prompt_kernel(q_ref, k_ref, v_ref, glow_ref, wgu_ref, bg_ref, o_ref, s_ref,
                       g_scr, st_scr):
    seq = q_ref.shape[0]
    c = GLA_CHUNK
    x = _dot_x3(glow_ref[...], wgu_ref[...]) + bg_ref[...]
    g_scr[...] = _log_sigmoid(x) / GLA_TAU
    st_scr[...] = jnp.zeros_like(st_scr)
    r_i = lax.broadcasted_iota(jnp.int32, (c, c), 0)
    c_i = lax.broadcasted_iota(jnp.int32, (c, c), 1)
    causal = c_i <= r_i
    tril = jnp.where(causal, 1.0, 0.0).astype(BF16)

    def group(gi, carry):
        rows = [pl.ds(pl.multiple_of((gi * GLA_GROUP + j) * c, c), c) for j in range(GLA_GROUP)]
        cums = []
        for r in rows:
            g_hi, g_mid, g_lo = _split3(g_scr[r, :])
            cums.append(_dot(tril, g_hi) + (_dot(tril, g_mid) + _dot(tril, g_lo)))
        work = []
        for r, b in zip(rows, cums):
            q = q_ref[r, :] * (GLA_DK ** -0.5)
            k = k_ref[r, :]
            v = v_ref[r, :].astype(BF16)
            b_mid = b[c // 2:c // 2 + 1, :]
            b_last = b[c - 1:c, :]
            q_in = (q * jnp.exp(b)).astype(BF16)
            q_mid = (q * jnp.exp(b - b_mid)).astype(BF16)
            k_mid = (k * jnp.exp(b_mid - b)).astype(BF16)
            k_out = (k * jnp.exp(b_last - b)).astype(BF16)
            att = jnp.where(causal, _dot_nt(q_mid, k_mid), 0.0).astype(BF16)
            work.append((r, q_in, att, v, jnp.exp(b_last), _dot_tn(v, k_out)))
        st = st_scr[...]
        for r, q_in, att, v, decay, update in work:
            o_ref[r, :] = _dot_nt(q_in, st.astype(BF16)) + _dot(att, v)
            st = st * decay + update
        st_scr[...] = st
        return carry

    lax.fori_loop(0, seq // (c * GLA_GROUP), group, 0)
    s_ref[0, 0] = jnp.transpose(st_scr[...])


def gla_prompt(qk, v, glow, wgu, bg, batch, seq):
    return pl.pallas_call(
        _gla_prompt_kernel,
        grid=(batch, GLA_HEADS),
        in_specs=[pl.BlockSpec((seq, GLA_DK), lambda b, h: (b, h)),
                  pl.BlockSpec((seq, GLA_DK), lambda b, h: (b, GLA_HEADS + h)),
                  pl.BlockSpec((seq, GLA_DV), lambda b, h: (b, h)),
                  pl.BlockSpec((seq, LANE), lambda b, h: (b, 0)),
                  pl.BlockSpec((LANE, GLA_DK), lambda b, h: (0, h)),
                  pl.BlockSpec((1, GLA_DK), lambda b, h: (0, h))],
        out_specs=[pl.BlockSpec((seq, GLA_DV), lambda b, h: (b, h)),
                   pl.BlockSpec((1, 1, GLA_DK, GLA_DV), lambda b, h: (b, h, 0, 0))],
        out_shape=[jax.ShapeDtypeStruct((batch * seq, GLA_VW), F32),
                   jax.ShapeDtypeStruct((batch, GLA_HEADS, GLA_DK, GLA_DV), F32)],
        scratch_shapes=[pltpu.VMEM((seq, GLA_DK), F32), pltpu.VMEM((GLA_DV, GLA_DK), F32)],
        compiler_params=_params("parallel", "parallel"),
        name="gla_prompt",
    )(qk, qk, v, glow, wgu, bg)


def _block_select_kernel(pt_ref, q_ref, *refs):
    pages, idx_ref, sums_scr = refs[:PAGES_PER_STEP], refs[-2], refs[-1]
    per_block = MOBA_BLOCK // PAGE_SIZE
    blocks_per_step = PAGES_PER_STEP // per_block
    s = pl.program_id(1)
    for j in range(blocks_per_step):
        tot = jnp.sum(pages[per_block * j][...], axis=0)
        for r in range(1, per_block):
            tot = tot + jnp.sum(pages[per_block * j + r][...], axis=0)
        sums_scr[s * blocks_per_step + j] = tot

    @pl.when(s == pl.num_programs(1) - 1)
    def _():
        nb = sums_scr.shape[0]
        q = q_ref[0] * (ATT_HD ** -0.5)
        gate = jnp.sum(sums_scr[...] * q[None], axis=-1) / MOBA_BLOCK
        blk = lax.broadcasted_iota(jnp.int32, (nb, ATT_HEADS), 0)
        for j in range(MOBA_TOPK):
            best = jnp.max(gate, axis=0, keepdims=True)
            idx = jnp.min(jnp.where(gate == best, blk, nb), axis=0, keepdims=True)
            idx_ref[0, j:j + 1, :] = idx
            gate = jnp.where(blk == idx, NEG_INF, gate)


def block_select(cache_k, q_s, page_table_flat, n_pages):
    n_seq = q_s.shape[0]
    steps = n_pages // PAGES_PER_STEP
    n_blocks = n_pages * PAGE_SIZE // MOBA_BLOCK

    def page_map(r, b, s, pt):
        return (0, pt[b * n_pages + s * PAGES_PER_STEP + r], 0, 0, 0)

    page_specs = [pl.BlockSpec((None, None, PAGE_SIZE, ATT_HEADS, ATT_HD), functools.partial(page_map, r))
                  for r in range(PAGES_PER_STEP)]
    return pl.pallas_call(
        _block_select_kernel,
        grid_spec=pltpu.PrefetchScalarGridSpec(
            num_scalar_prefetch=1,
            grid=(n_seq, steps),
            in_specs=[pl.BlockSpec((1, ATT_HEADS, ATT_HD), lambda b, s, pt: (b, 0, 0))] + page_specs,
            out_specs=pl.BlockSpec((1, MOBA_TOPK, ATT_HEADS), lambda b, s, pt: (b, 0, 0)),
            scratch_shapes=[pltpu.VMEM((n_blocks, ATT_HEADS, ATT_HD), F32)],
        ),
        out_shape=jax.ShapeDtypeStruct((n_seq, MOBA_TOPK, ATT_HEADS), jnp.int32),
        compiler_params=_params("parallel", "arbitrary"),
        name="block_select",
    )(page_table_flat, q_s, *([cache_k] * PAGES_PER_STEP))


def _moba_sample_kernel(sel_ref, pt_ref, slopes_ref, q_ref, kn_ref, vn_ref, ck_ref, cv_ref, o_ref,
                        kbuf, vbuf, sems, *, n_pages):
    per_block = MOBA_BLOCK // PAGE_SIZE
    past = n_pages * PAGE_SIZE
    b = pl.program_id(0)

    def gather(seq, slot):
        copies = []
        for h in range(ATT_HEADS):
            for j in range(MOBA_TOPK):
                blk = sel_ref[(seq * MOBA_TOPK + j) * ATT_HEADS + h]
                for half in range(per_block):
                    page = pt_ref[seq * n_pages + blk * per_block + half]
                    rows = pl.ds((j * per_block + half) * PAGE_SIZE, PAGE_SIZE)
                    copies.append(pltpu.make_async_copy(
                        ck_ref.at[0, page, :, h, :], kbuf.at[slot, h, rows, :], sems.at[0, slot]))
                    copies.append(pltpu.make_async_copy(
                        cv_ref.at[0, page, :, h, :], vbuf.at[slot, h, rows, :], sems.at[1, slot]))
        return copies

    @pl.when(b == 0)
    def _():
        for cp in gather(0, 0):
            cp.start()

    @pl.when(b + 1 < pl.num_programs(0))
    def _():
        for cp in gather(b + 1, (b + 1) % 2):
            cp.start()

    slot = b % 2
    for cp in gather(b, slot):
        cp.wait()

    row = lax.broadcasted_iota(jnp.int32, (MOBA_BLOCK, 1), 0)
    for h in range(ATT_HEADS):
        q = q_ref[0, h:h + 1, :] * (ATT_HD ** -0.5)
        s = jnp.sum(kbuf[slot, h] * q, axis=-1, keepdims=True)
        dist = [(past - sel_ref[(b * MOBA_TOPK + j) * ATT_HEADS + h] * MOBA_BLOCK - row).astype(F32)
                for j in range(MOBA_TOPK)]
        s = s - slopes_ref[h] * jnp.concatenate(dist, axis=0)
        s_own = jnp.sum(q * kn_ref[0, h:h + 1, :], axis=-1, keepdims=True)
        m = jnp.maximum(jnp.max(s, axis=0, keepdims=True), s_own)
        p = jnp.exp(s - m)
        p_own = jnp.exp(s_own - m)
        denom = jnp.sum(p, axis=0, keepdims=True) + p_own
        acc = jnp.sum(p * vbuf[slot, h], axis=0, keepdims=True) + p_own * vn_ref[0, h:h + 1, :]
        o_ref[0, h:h + 1, :] = acc / denom


def moba_sample(q_s, k_new, v_new, cache_k, cache_v, sel_flat, page_table_flat, slopes, n_pages):
    n_seq = q_s.shape[0]
    tok = pl.BlockSpec((1, ATT_HEADS, ATT_HD), lambda b, sel, pt: (b, 0, 0))
    buf = pltpu.VMEM((2, ATT_HEADS, MOBA_TOPK * MOBA_BLOCK, ATT_HD), F32)
    return pl.pallas_call(
        functools.partial(_moba_sample_kernel, n_pages=n_pages),
        grid_spec=pltpu.PrefetchScalarGridSpec(
            num_scalar_prefetch=2,
            grid=(n_seq,),
            in_specs=[pl.BlockSpec(memory_space=pltpu.SMEM), tok, tok, tok,
                      pl.BlockSpec(memory_space=pl.ANY), pl.BlockSpec(memory_space=pl.ANY)],
            out_specs=tok,
            scratch_shapes=[buf, buf, pltpu.SemaphoreType.DMA((2, 2))],
        ),
        out_shape=jax.ShapeDtypeStruct(q_s.shape, F32),
        compiler_params=_params("arbitrary"),
        name="moba_sample",
    )(sel_flat, page_table_flat, slopes, q_s, k_new, v_new, cache_k, cache_v)


def _to_column(row, eye):
    return jnp.sum(jnp.where(eye, row, 0.0), axis=-1, keepdims=True)


def _gla_sample_kernel(q_ref, k_ref, v_ref, glow_ref, wgu_ref, bg_ref, s0_ref, o_ref, s_ref):
    eye = (lax.broadcasted_iota(jnp.int32, (GLA_DK, GLA_DK), 0)
           == lax.broadcasted_iota(jnp.int32, (GLA_DK, GLA_DK), 1))
    glow8 = jnp.broadcast_to(glow_ref[0], (8, LANE))
    x = _dot_x3(glow8, wgu_ref[...])[0:1, :] + bg_ref[...]
    a = jnp.exp(_log_sigmoid(x) / GLA_TAU)
    for h in range(GLA_HEADS):
        kc = slice(h * GLA_DK, (h + 1) * GLA_DK)
        vc = slice(h * GLA_DV, (h + 1) * GLA_DV)
        a_col = _to_column(a[:, kc], eye)
        k_col = _to_column(k_ref[0][:, kc], eye)
        q_col = _to_column(q_ref[0][:, kc] * (GLA_DK ** -0.5), eye)
        s_new = a_col * s0_ref[0, h] + k_col * v_ref[0][:, vc]
        s_ref[0, h] = s_new
        o_ref[0, :, vc] = jnp.sum(q_col * s_new, axis=0, keepdims=True)


def gla_sample(qk, v, glow, wgu, bg, s0):
    n_seq = qk.shape[0]
    st_spec = pl.BlockSpec((1, GLA_HEADS, GLA_DK, GLA_DV), lambda b: (b, 0, 0, 0))
    return pl.pallas_call(
        _gla_sample_kernel,
        grid=(n_seq,),
        in_specs=[pl.BlockSpec((1, 1, GLA_KW), lambda b: (b, 0, 0)),
                  pl.BlockSpec((1, 1, GLA_KW), lambda b: (b, 0, 1)),
                  pl.BlockSpec((1, 1, GLA_VW), lambda b: (b, 0, 0)),
                  pl.BlockSpec((1, 1, LANE), lambda b: (b, 0, 0)),
                  pl.BlockSpec((LANE, GLA_KW), lambda b: (0, 0)),
                  pl.BlockSpec((1, GLA_KW), lambda b: (0, 0)),
                  st_spec],
        out_specs=[pl.BlockSpec((1, 1, GLA_VW), lambda b: (b, 0, 0)), st_spec],
        out_shape=[jax.ShapeDtypeStruct((n_seq, 1, GLA_VW), F32),
                   jax.ShapeDtypeStruct(s0.shape, F32)],
        compiler_params=_params("parallel"),
        name="gla_sample",
    )(qk, qk, v, glow, wgu, bg, s0)


def _input_projections(x, norm_g, w_main, w_glow, tm):
    xn = rms_norm_bf16(x, norm_g, tm)
    outs = [project(xn, w_main, c, ATT_WIDTH, tm) for c in range(w_main.shape[1] // ATT_WIDTH)]
    outs.append(project(xn, w_glow, 0, LANE, tm))
    return outs


def kernel(x_prompt, x_sample, cache_k, cache_v, state_gla, page_table, attn_norm_g, w_in,
           w_gate_up, b_gate, att_out_norm_g, gla_norm_g, w_o, ffn_norm_g, w_ffn_gate,
           w_ffn_up, w_ffn_down, final_norm_g):
    depth = w_in.shape[0]
    assert depth == 1, "single-layer trunk"
    batch, seq, d = x_prompt.shape
    n_seq, dec_seq, _ = x_sample.shape
    assert dec_seq == 1 and seq % MOBA_BLOCK == 0 and seq % GLA_CHUNK == 0
    n_pages = page_table.shape[1]
    assert (n_pages * PAGE_SIZE) % MOBA_BLOCK == 0 and n_pages % PAGES_PER_STEP == 0
    assert n_pages * PAGE_SIZE // MOBA_BLOCK >= MOBA_TOPK
    assert seq // MOBA_BLOCK == SUBLANE, "prompt MoBA packs one block-bias row per sublane"

    slopes = jnp.exp2(-8.0 * jnp.arange(1, ATT_HEADS + 1, dtype=F32) / ATT_HEADS)
    main_w = 3 * ATT_WIDTH + 2 * GLA_KW + 2 * GLA_VW
    w_in_b = w_in[0].astype(BF16)
    w_main = w_in_b[:, :main_w]
    w_glow = jnp.pad(w_in_b[:, main_w:], ((0, 0), (0, LANE - GLA_GATE_RANK)))
    wgu = jnp.pad(w_gate_up[0], ((0, LANE - GLA_GATE_RANK), (0, 0)))
    bg = b_gate[0].reshape(1, GLA_KW)
    w_o_b = w_o[0].astype(BF16)
    w_fg = w_ffn_gate[0].astype(BF16)
    w_fu = w_ffn_up[0].astype(BF16)
    w_fd = w_ffn_down[0].astype(BF16)

    xp = x_prompt.reshape(batch * seq, d)
    q_a, k_a, v_a, qk_g, v_g, r_g, glow = _input_projections(xp, attn_norm_g[0], w_main, w_glow, 512)
    att = moba_prompt(q_a, k_a, v_a, slopes, batch, seq)
    gla_o, st_p = gla_prompt(qk_g, v_g, glow, wgu, bg, batch, seq)
    h_p = out_project(xp, att, gla_o, r_g, att_out_norm_g[0], gla_norm_g[0], w_o_b, 256)
    y_p = ffn_final_norm(h_p, ffn_norm_g[0], w_fg, w_fu, w_fd, final_norm_g, 512, 512)

    xs = x_sample.reshape(n_seq, d)
    q_s, k_s, v_s, qk_s, vg_s, r_s, glow_s = _input_projections(xs, attn_norm_g[0], w_main, w_glow, n_seq)
    pt_flat = page_table.reshape(-1)
    tok3 = lambda a: a.reshape(n_seq, 1, a.shape[-1])
    heads3 = lambda a: a.reshape(n_seq, ATT_HEADS, ATT_HD)
    sel = block_select(cache_k, heads3(q_s), pt_flat, n_pages)
    att_s = moba_sample(heads3(q_s), heads3(k_s), heads3(v_s), cache_k, cache_v, sel.reshape(-1),
                        pt_flat, slopes, n_pages)
    gla_s, st_s = gla_sample(tok3(qk_s), tok3(vg_s), tok3(glow_s), wgu, bg, state_gla[0])
    h_s = out_project(xs, att_s.reshape(n_seq, ATT_WIDTH), gla_s.reshape(n_seq, GLA_VW), r_s,
                      att_out_norm_g[0], gla_norm_g[0], w_o_b, n_seq)
    y_s = ffn_final_norm(h_s, ffn_norm_g[0], w_fg, w_fu, w_fd, final_norm_g, n_seq, 512)

    kv_p = (depth, batch, seq, ATT_HEADS, ATT_HD)
    kv_s = (depth, n_seq, dec_seq, ATT_HEADS, ATT_HD)
    return (y_p.reshape(batch, seq, d), y_s.reshape(n_seq, dec_seq, d),
            k_a.reshape(kv_p), v_a.reshape(kv_p), k_s.reshape(kv_s), v_s.reshape(kv_s),
            st_p[None], st_s[None])
```

```python
import functools

import jax
import jax.numpy as jnp
from jax import lax
from jax.experimental import pallas as pl
from jax.experimental.pallas import tpu as pltpu

D_MODEL = 2048
PAGE_SIZE = 128
ATT_HEADS = 8
ATT_HD = 128
ATT_WIDTH = ATT_HEADS * ATT_HD
MOBA_BLOCK = 256
MOBA_TOPK = 3
GLA_HEADS = 4
GLA_DK = 128
GLA_DV = 256
GLA_KW = GLA_HEADS * GLA_DK
GLA_VW = GLA_HEADS * GLA_DV
GLA_GATE_RANK = 16
GLA_TAU = 16.0
GLA_CHUNK = 64
MAIN_WIDTH = 3 * ATT_WIDTH + 2 * GLA_KW + 2 * GLA_VW
EPS = 1e-6

LANE = 128
SUBLANE = 8
VMEM_LIMIT = 48 * 1024 * 1024
VMEM_LIMIT_FFN_PAGES = 56 * 1024 * 1024
GLA_GROUP = 8

F32 = jnp.float32
BF16 = jnp.bfloat16
NEG_INF = float("-inf")
LOG2E = 1.4426950408889634
MASKED = -1e30


def _params(*sem):
    return pltpu.CompilerParams(dimension_semantics=sem, vmem_limit_bytes=VMEM_LIMIT)


def _dot(a, b, precision=None):
    return jnp.dot(a, b, preferred_element_type=F32, precision=precision)


def _dot_nt(a, b, precision=None):
    return lax.dot_general(a, b, (((1,), (1,)), ((), ())),
                           preferred_element_type=F32, precision=precision)


def _dot_tn(a, b, precision=None):
    return lax.dot_general(a, b, (((0,), (0,)), ((), ())),
                           preferred_element_type=F32, precision=precision)


def _split2(x):
    hi = x.astype(BF16)
    return hi, (x - hi.astype(F32)).astype(BF16)


def _split3(x):
    hi = x.astype(BF16)
    r = x - hi.astype(F32)
    mid = r.astype(BF16)
    return hi, mid, (r - mid.astype(F32)).astype(BF16)


def _dot_x3(a, b):
    a_hi, a_lo = _split2(a)
    b_hi, b_lo = _split2(b)
    return _dot(a_hi, b_hi) + (_dot(a_hi, b_lo) + _dot(a_lo, b_hi))


def _rms(x, g):
    return x * lax.rsqrt(jnp.mean(x * x, axis=-1, keepdims=True) + EPS) * g


def _silu(x):
    return x * (1.0 / (1.0 + jnp.exp(-x)))


def _log_sigmoid(x):
    return jnp.minimum(x, 0.0) - jnp.log1p(jnp.exp(-jnp.abs(x)))


def _norm_kernel(x_ref, g_ref, o_ref):
    o_ref[...] = _rms(x_ref[...], g_ref[...]).astype(o_ref.dtype)


def rms_norm_bf16(x, g, tm):
    m, d = x.shape
    return pl.pallas_call(
        _norm_kernel,
        grid=(m // tm,),
        in_specs=[pl.BlockSpec((tm, d), lambda i: (i, 0)),
                  pl.BlockSpec((1, d), lambda i: (0, 0))],
        out_specs=pl.BlockSpec((tm, d), lambda i: (i, 0)),
        out_shape=jax.ShapeDtypeStruct((m, d), BF16),
        compiler_params=_params("parallel"),
        name="rms_norm",
    )(x, g.reshape(1, d))


def _proj_kernel(x_ref, w_ref, o_ref):
    o_ref[...] = _dot(x_ref[...], w_ref[...])


def project(xn, w, col_block, tn, tm):
    m, k = xn.shape
    return pl.pallas_call(
        _proj_kernel,
        grid=(m // tm,),
        in_specs=[pl.BlockSpec((tm, k), lambda i: (i, 0)),
                  pl.BlockSpec((k, tn), lambda i: (0, col_block))],
        out_specs=pl.BlockSpec((tm, tn), lambda i: (i, 0)),
        out_shape=jax.ShapeDtypeStruct((m, tn), F32),
        compiler_params=_params("parallel"),
        name="in_proj",
    )(xn, w)


def _out_proj_kernel(x_ref, att_ref, gla_ref, r_ref, ag_ref, gg_ref, w_ref, o_ref):
    parts = [_rms(att_ref[...], ag_ref[...]).astype(BF16)]
    for h in range(GLA_HEADS):
        cols = slice(h * GLA_DV, (h + 1) * GLA_DV)
        y = _rms(gla_ref[:, cols], gg_ref[...]) * _silu(r_ref[:, cols])
        parts.append(y.astype(BF16))
    y = jnp.concatenate(parts, axis=-1)
    o_ref[...] = x_ref[...] + _dot(y, w_ref[...])


def out_project(x, att, gla_o, r_g, att_g, gla_g, w_o, tm):
    m, d = x.shape
    row = lambda i: (i, 0)
    fixed = lambda i: (0, 0)
    return pl.pallas_call(
        _out_proj_kernel,
        grid=(m // tm,),
        in_specs=[pl.BlockSpec((tm, d), row),
                  pl.BlockSpec((tm, ATT_WIDTH), row),
                  pl.BlockSpec((tm, GLA_VW), row),
                  pl.BlockSpec((tm, GLA_VW), row),
                  pl.BlockSpec((1, ATT_WIDTH), fixed),
                  pl.BlockSpec((1, GLA_DV), fixed),
                  pl.BlockSpec(w_o.shape, fixed)],
        out_specs=pl.BlockSpec((tm, d), row),
        out_shape=jax.ShapeDtypeStruct((m, d), F32),
        compiler_params=_params("parallel"),
        name="out_proj",
    )(x, att, gla_o, r_g, att_g.reshape(1, -1), gla_g.reshape(1, -1), w_o)


def _ffn_start(h_ref, g_ref, hn_scr, acc_scr):
    @pl.when(pl.program_id(1) == 0)
    def _():
        hn_scr[...] = _rms(h_ref[...], g_ref[...]).astype(BF16)
        acc_scr[...] = jnp.zeros_like(acc_scr)


def _ffn_step(wg_ref, wu_ref, wd_ref, hn_scr, acc_scr):
    hn = hn_scr[...]
    act = _silu(_dot(hn, wg_ref[...])) * _dot(hn, wu_ref[...])
    acc_scr[...] += _dot(act.astype(BF16), wd_ref[...])


def _ffn_finish(h_ref, fg_ref, o_ref, acc_scr):
    @pl.when(pl.program_id(1) == pl.num_programs(1) - 1)
    def _():
        o_ref[...] = _rms(h_ref[...] + acc_scr[...], fg_ref[...])


def _ffn_kernel(h_ref, g_ref, wg_ref, wu_ref, wd_ref, fg_ref, o_ref, hn_scr, acc_scr):
    _ffn_start(h_ref, g_ref, hn_scr, acc_scr)
    _ffn_step(wg_ref, wu_ref, wd_ref, hn_scr, acc_scr)
    _ffn_finish(h_ref, fg_ref, o_ref, acc_scr)


def _ffn_specs(m, d, d_ff, tm, tf):
    in_specs = [pl.BlockSpec((tm, d), lambda i, f, *_: (i, 0)),
                pl.BlockSpec((1, d), lambda i, f, *_: (0, 0)),
                pl.BlockSpec((d, tf), lambda i, f, *_: (0, f)),
                pl.BlockSpec((d, tf), lambda i, f, *_: (0, f)),
                pl.BlockSpec((tf, d), lambda i, f, *_: (f, 0)),
                pl.BlockSpec((1, d), lambda i, f, *_: (0, 0))]
    out_spec = pl.BlockSpec((tm, d), lambda i, f, *_: (i, 0))
    scratch = [pltpu.VMEM((tm, d), BF16), pltpu.VMEM((tm, d), F32)]
    return (m // tm, d_ff // tf), in_specs, out_spec, scratch


def ffn_final_norm(h, ffn_g, w_gate, w_up, w_down, final_g, tm, tf):
    m, d = h.shape
    grid, in_specs, out_spec, scratch = _ffn_specs(m, d, w_gate.shape[1], tm, tf)
    return pl.pallas_call(
        _ffn_kernel,
        grid=grid,
        in_specs=in_specs,
        out_specs=out_spec,
        out_shape=jax.ShapeDtypeStruct((m, d), F32),
        scratch_shapes=scratch,
        compiler_params=_params("parallel", "arbitrary"),
        name="ffn",
    )(h, ffn_g.reshape(1, d), w_gate, w_up, w_down, final_g.reshape(1, d))


def _ffn_pages_kernel(pt_ref, h_ref, g_ref, wg_ref, wu_ref, wd_ref, fg_ref, ck_ref, o_ref, sums_ref,
                      hn_scr, acc_scr, pbuf, sems, *, pages_per_step):
    n_f = pl.num_programs(1)
    t = pl.program_id(0) * n_f + pl.program_id(1)

    def fetch(step, slot):
        return [pltpu.make_async_copy(ck_ref.at[0, pt_ref[step * pages_per_step + r]],
                                      pbuf.at[slot, r], sems.at[slot])
                for r in range(pages_per_step)]

    @pl.when(t == 0)
    def _():
        for cp in fetch(0, 0):
            cp.start()

    @pl.when(t + 1 < pl.num_programs(0) * n_f)
    def _():
        for cp in fetch(t + 1, (t + 1) % 2):
            cp.start()

    slot = t % 2
    for cp in fetch(t, slot):
        cp.wait()

    _ffn_start(h_ref, g_ref, hn_scr, acc_scr)
    per_block = MOBA_BLOCK // PAGE_SIZE
    for j in range(pages_per_step // per_block):
        tot = jnp.sum(pbuf[slot, per_block * j], axis=0)
        for r in range(1, per_block):
            tot = tot + jnp.sum(pbuf[slot, per_block * j + r], axis=0)
        sums_ref[j] = tot
    _ffn_step(wg_ref, wu_ref, wd_ref, hn_scr, acc_scr)
    _ffn_finish(h_ref, fg_ref, o_ref, acc_scr)


def ffn_final_norm_with_page_sums(h, ffn_g, w_gate, w_up, w_down, final_g, cache_k, page_table_flat,
                                  tm, tf):
    m, d = h.shape
    grid, in_specs, out_spec, scratch = _ffn_specs(m, d, w_gate.shape[1], tm, tf)
    n_steps = grid[0] * grid[1]
    per_block = MOBA_BLOCK // PAGE_SIZE
    n_pages = page_table_flat.shape[0]
    pages_per_step = per_block * (-(-n_pages // (per_block * n_steps)))
    blocks_per_step = pages_per_step // per_block
    pt = jnp.pad(page_table_flat, (0, n_steps * pages_per_step - n_pages), mode="edge")
    page_shape = cache_k.shape[2:]
    return pl.pallas_call(
        functools.partial(_ffn_pages_kernel, pages_per_step=pages_per_step),
        grid_spec=pltpu.PrefetchScalarGridSpec(
            num_scalar_prefetch=1,
            grid=grid,
            in_specs=in_specs + [pl.BlockSpec(memory_space=pl.ANY)],
            out_specs=[out_spec,
                       pl.BlockSpec((blocks_per_step,) + page_shape[1:],
                                    lambda i, f, pt: (i * grid[1] + f, 0, 0))],
            scratch_shapes=scratch + [pltpu.VMEM((2, pages_per_step) + page_shape, F32),
                                      pltpu.SemaphoreType.DMA((2,))],
        ),
        out_shape=[jax.ShapeDtypeStruct((m, d), F32),
                   jax.ShapeDtypeStruct((n_steps * blocks_per_step,) + page_shape[1:], F32)],
        compiler_params=pltpu.CompilerParams(dimension_semantics=("arbitrary", "arbitrary"),
                                             vmem_limit_bytes=VMEM_LIMIT_FFN_PAGES),
        name="ffn_pages",
    )(pt, h, ffn_g.reshape(1, d), w_gate, w_up, w_down, final_g.reshape(1, d), cache_k)


def _moba_prompt_kernel(slopes_ref, q_ref, k_ref, v_ref, o_ref):
    seq = q_ref.shape[0]
    nb = seq // MOBA_BLOCK
    slope2 = slopes_ref[pl.program_id(1)] * LOG2E
    k = k_ref[...]
    means = [jnp.sum(k[n * MOBA_BLOCK:(n + 1) * MOBA_BLOCK], axis=0, keepdims=True) / MOBA_BLOCK
             for n in range(nb)]
    mean_hi, mean_lo = _split2(jnp.concatenate(means, axis=0))
    row = lax.broadcasted_iota(jnp.int32, (seq, LANE), 0)
    lane = lax.broadcasted_iota(jnp.int32, (seq, LANE), 1)
    b_hi, b_mid, b_lo = _split3(slope2 * row.astype(F32))
    ext = jnp.where(lane == lax.shift_right_logical(row, MOBA_BLOCK.bit_length() - 1), 1.0, 0.0)
    ext = ext.astype(BF16)
    ext = jnp.where(lane == nb, b_hi, ext)
    ext = jnp.where(lane == nb + 1, b_mid, ext)
    ext = jnp.where(lane == nb + 2, b_lo, ext)
    k_ext = jnp.concatenate([k.astype(BF16), ext], axis=-1)
    v_t = jnp.transpose(v_ref[...]).astype(BF16)
    blk_id = lax.broadcasted_iota(jnp.int32, (nb, MOBA_BLOCK), 0)
    ext_row = lax.broadcasted_iota(jnp.int32, (LANE, MOBA_BLOCK), 0)
    ones_rows = jnp.where((ext_row >= nb) & (ext_row < nb + 3), 1.0, 0.0)
    key_loc = lax.broadcasted_iota(jnp.int32, (MOBA_BLOCK, MOBA_BLOCK), 0)
    qry_loc = lax.broadcasted_iota(jnp.int32, (MOBA_BLOCK, MOBA_BLOCK), 1)

    def scores(i):
        q = q_ref[i * MOBA_BLOCK:(i + 1) * MOBA_BLOCK, :] * (ATT_HD ** -0.5 * LOG2E)
        q_t = jnp.transpose(q)
        q_ext = ones_rows
        if i > MOBA_TOPK:
            qt_hi, qt_lo = _split2(q_t)
            gate_t = _dot(mean_hi, qt_hi) + (_dot(mean_hi, qt_lo) + _dot(mean_lo, qt_hi))
            rank = jnp.zeros((nb, MOBA_BLOCK), jnp.int32)
            for m in range(i):
                gm = gate_t[m:m + 1, :]
                beats = (gm > gate_t) | ((gm == gate_t) & (blk_id > m))
                rank = rank + beats.astype(jnp.int32)
            bias_t = jnp.where((rank < MOBA_TOPK) | (blk_id >= i), 0.0, MASKED)
            bias_t = jnp.concatenate([bias_t, jnp.zeros((LANE - nb, MOBA_BLOCK), F32)], axis=0)
            q_ext = bias_t + ones_rows
        q_full = jnp.concatenate([q_t.astype(BF16), q_ext.astype(BF16)], axis=0)
        return _dot(k_ext[:(i + 1) * MOBA_BLOCK], q_full)

    s_next = scores(0)
    for i in range(nb):
        s_t = s_next
        if i + 1 < nb:
            s_next = scores(i + 1)
        own = jnp.where(key_loc <= qry_loc, s_t[i * MOBA_BLOCK:], NEG_INF)
        s_t = own if i == 0 else jnp.concatenate([s_t[:i * MOBA_BLOCK], own], axis=0)
        m_row = jnp.max(s_t, axis=0, keepdims=True)
        p = jnp.exp2(s_t - m_row)
        l_row = jnp.sum(p, axis=0, keepdims=True)
        o_t = _dot(v_t[:, :(i + 1) * MOBA_BLOCK], p.astype(BF16))
        o_ref[i * MOBA_BLOCK:(i + 1) * MOBA_BLOCK, :] = jnp.transpose(o_t * (1.0 / l_row))


def moba_prompt(q, k, v, slopes, batch, seq):
    spec = pl.BlockSpec((seq, ATT_HD), lambda b, h: (b, h))
    return pl.pallas_call(
        _moba_prompt_kernel,
        grid=(batch, ATT_HEADS),
        in_specs=[pl.BlockSpec(memory_space=pltpu.SMEM), spec, spec, spec],
        out_specs=spec,
        out_shape=jax.ShapeDtypeStruct(q.shape, F32),
        compiler_params=_params("parallel", "parallel"),
        name="moba_prompt",
    )(slopes, q, k, v)


def _gla_prompt_kernel(q_ref, k_ref, v_ref, glow_ref, wgu_ref, bg_ref, o_ref, s_ref,
                       g_scr, st_scr):
    seq = q_ref.shape[0]
    c = GLA_CHUNK
    x = _dot_x3(glow_ref[...], wgu_ref[...]) + bg_ref[...]
    g_scr[...] = _log_sigmoid(x) / GLA_TAU
    st_scr[...] = jnp.zeros_like(st_scr)
    r_i = lax.broadcasted_iota(jnp.int32, (c, c), 0)
    c_i = lax.broadcasted_iota(jnp.int32, (c, c), 1)
    causal = c_i <= r_i
    tril = jnp.where(causal, 1.0, 0.0).astype(BF16)

    def group(gi, carry):
        rows = [pl.ds(pl.multiple_of((gi * GLA_GROUP + j) * c, c), c) for j in range(GLA_GROUP)]
        cums = []
        for r in rows:
            g_hi, g_mid, g_lo = _split3(g_scr[r, :])
            cums.append(_dot(tril, g_hi) + (_dot(tril, g_mid) + _dot(tril, g_lo)))
        work = []
        for r, b in zip(rows, cums):
            q = q_ref[r, :] * (GLA_DK ** -0.5)
            k = k_ref[r, :]
            v = v_ref[r, :].astype(BF16)
            b_mid = b[c // 2:c // 2 + 1, :]
            b_last = b[c - 1:c, :]
            q_in = (q * jnp.exp(b)).astype(BF16)
            q_mid = (q * jnp.exp(b - b_mid)).astype(BF16)
            k_mid = (k * jnp.exp(b_mid - b)).astype(BF16)
            k_out = (k * jnp.exp(b_last - b)).astype(BF16)
            att = jnp.where(causal, _dot_nt(q_mid, k_mid), 0.0).astype(BF16)
            work.append((r, q_in, att, v, jnp.exp(b_last), _dot_tn(v, k_out)))
        st = st_scr[...]
        for r, q_in, att, v, decay, update in work:
            o_ref[r, :] = _dot_nt(q_in, st.astype(BF16)) + _dot(att, v)
            st = st * decay + update
        st_scr[...] = st
        return carry

    lax.fori_loop(0, seq // (c * GLA_GROUP), group, 0)
    s_ref[0, 0] = jnp.transpose(st_scr[...])


def gla_prompt(qk, v, glow, wgu, bg, batch, seq):
    return pl.pallas_call(
        _gla_prompt_kernel,
        grid=(batch, GLA_HEADS),
        in_specs=[pl.BlockSpec((seq, GLA_DK), lambda b, h: (b, h)),
                  pl.BlockSpec((seq, GLA_DK), lambda b, h: (b, GLA_HEADS + h)),
                  pl.BlockSpec((seq, GLA_DV), lambda b, h: (b, h)),
                  pl.BlockSpec((seq, LANE), lambda b, h: (b, 0)),
                  pl.BlockSpec((LANE, GLA_DK), lambda b, h: (0, h)),
                  pl.BlockSpec((1, GLA_DK), lambda b, h: (0, h))],
        out_specs=[pl.BlockSpec((seq, GLA_DV), lambda b, h: (b, h)),
                   pl.BlockSpec((1, 1, GLA_DK, GLA_DV), lambda b, h: (b, h, 0, 0))],
        out_shape=[jax.ShapeDtypeStruct((batch * seq, GLA_VW), F32),
                   jax.ShapeDtypeStruct((batch, GLA_HEADS, GLA_DK, GLA_DV), F32)],
        scratch_shapes=[pltpu.VMEM((seq, GLA_DK), F32), pltpu.VMEM((GLA_DV, GLA_DK), F32)],
        compiler_params=_params("parallel", "parallel"),
        name="gla_prompt",
    )(qk, qk, v, glow, wgu, bg)


def _block_select_kernel(sums_ref, q_ref, idx_ref):
    nb = sums_ref.shape[0]
    q = q_ref[0] * (ATT_HD ** -0.5)
    gate = jnp.sum(sums_ref[...] * q[None], axis=-1) / MOBA_BLOCK
    blk = lax.broadcasted_iota(jnp.int32, (nb, ATT_HEADS), 0)
    for j in range(MOBA_TOPK):
        best = jnp.max(gate, axis=0, keepdims=True)
        idx = jnp.min(jnp.where(gate == best, blk, nb), axis=0, keepdims=True)
        idx_ref[0, j:j + 1, :] = idx
        gate = jnp.where(blk == idx, NEG_INF, gate)


def block_select(sums, q_s, n_blocks):
    n_seq = q_s.shape[0]
    return pl.pallas_call(
        _block_select_kernel,
        grid=(n_seq,),
        in_specs=[pl.BlockSpec((n_blocks, ATT_HEADS, ATT_HD), lambda b: (b, 0, 0)),
                  pl.BlockSpec((1, ATT_HEADS, ATT_HD), lambda b: (b, 0, 0))],
        out_specs=pl.BlockSpec((1, MOBA_TOPK, ATT_HEADS), lambda b: (b, 0, 0)),
        out_shape=jax.ShapeDtypeStruct((n_seq, MOBA_TOPK, ATT_HEADS), jnp.int32),
        compiler_params=_params("parallel"),
        name="block_select",
    )(sums, q_s)


def _moba_sample_kernel(sel_ref, pt_ref, slopes_ref, q_ref, kn_ref, vn_ref, ck_ref, cv_ref, o_ref,
                        kbuf, vbuf, sems, *, n_pages):
    per_block = MOBA_BLOCK // PAGE_SIZE
    past = n_pages * PAGE_SIZE
    b = pl.program_id(0)

    def gather(seq, slot):
        copies = []
        for h in range(ATT_HEADS):
            for j in range(MOBA_TOPK):
                blk = sel_ref[(seq * MOBA_TOPK + j) * ATT_HEADS + h]
                for half in range(per_block):
                    page = pt_ref[seq * n_pages + blk * per_block + half]
                    rows = pl.ds((j * per_block + half) * PAGE_SIZE, PAGE_SIZE)
                    copies.append(pltpu.make_async_copy(
                        ck_ref.at[0, page, :, h, :], kbuf.at[slot, h, rows, :], sems.at[0, slot]))
                    copies.append(pltpu.make_async_copy(
                        cv_ref.at[0, page, :, h, :], vbuf.at[slot, h, rows, :], sems.at[1, slot]))
        return copies

    @pl.when(b == 0)
    def _():
        for cp in gather(0, 0):
            cp.start()

    @pl.when(b + 1 < pl.num_programs(0))
    def _():
        for cp in gather(b + 1, (b + 1) % 2):
            cp.start()

    slot = b % 2
    for cp in gather(b, slot):
        cp.wait()

    row = lax.broadcasted_iota(jnp.int32, (MOBA_BLOCK, 1), 0)
    for h in range(ATT_HEADS):
        q = q_ref[0, h:h + 1, :] * (ATT_HD ** -0.5)
        s = jnp.sum(kbuf[slot, h] * q, axis=-1, keepdims=True)
        dist = [(past - sel_ref[(b * MOBA_TOPK + j) * ATT_HEADS + h] * MOBA_BLOCK - row).astype(F32)
                for j in range(MOBA_TOPK)]
        s = s - slopes_ref[h] * jnp.concatenate(dist, axis=0)
        s_own = jnp.sum(q * kn_ref[0, h:h + 1, :], axis=-1, keepdims=True)
        m = jnp.maximum(jnp.max(s, axis=0, keepdims=True), s_own)
        p = jnp.exp(s - m)
        p_own = jnp.exp(s_own - m)
        denom = jnp.sum(p, axis=0, keepdims=True) + p_own
        acc = jnp.sum(p * vbuf[slot, h], axis=0, keepdims=True) + p_own * vn_ref[0, h:h + 1, :]
        o_ref[0, h:h + 1, :] = acc / denom


def moba_sample(q_s, k_new, v_new, cache_k, cache_v, sel_flat, page_table_flat, slopes, n_pages):
    n_seq = q_s.shape[0]
    tok = pl.BlockSpec((1, ATT_HEADS, ATT_HD), lambda b, sel, pt: (b, 0, 0))
    buf = pltpu.VMEM((2, ATT_HEADS, MOBA_TOPK * MOBA_BLOCK, ATT_HD), F32)
    return pl.pallas_call(
        functools.partial(_moba_sample_kernel, n_pages=n_pages),
        grid_spec=pltpu.PrefetchScalarGridSpec(
            num_scalar_prefetch=2,
            grid=(n_seq,),
            in_specs=[pl.BlockSpec(memory_space=pltpu.SMEM), tok, tok, tok,
                      pl.BlockSpec(memory_space=pl.ANY), pl.BlockSpec(memory_space=pl.ANY)],
            out_specs=tok,
            scratch_shapes=[buf, buf, pltpu.SemaphoreType.DMA((2, 2))],
        ),
        out_shape=jax.ShapeDtypeStruct(q_s.shape, F32),
        compiler_params=_params("arbitrary"),
        name="moba_sample",
    )(sel_flat, page_table_flat, slopes, q_s, k_new, v_new, cache_k, cache_v)


def _to_column(row, eye):
    return jnp.sum(jnp.where(eye, row, 0.0), axis=-1, keepdims=True)


def _gla_sample_kernel(q_ref, k_ref, v_ref, glow_ref, wgu_ref, bg_ref, s0_ref, o_ref, s_ref):
    eye = (lax.broadcasted_iota(jnp.int32, (GLA_DK, GLA_DK), 0)
           == lax.broadcasted_iota(jnp.int32, (GLA_DK, GLA_DK), 1))
    glow8 = jnp.broadcast_to(glow_ref[0], (8, LANE))
    x = _dot_x3(glow8, wgu_ref[...])[0:1, :] + bg_ref[...]
    a = jnp.exp(_log_sigmoid(x) / GLA_TAU)
    for h in range(GLA_HEADS):
        kc = slice(h * GLA_DK, (h + 1) * GLA_DK)
        vc = slice(h * GLA_DV, (h + 1) * GLA_DV)
        a_col = _to_column(a[:, kc], eye)
        k_col = _to_column(k_ref[0][:, kc], eye)
        q_col = _to_column(q_ref[0][:, kc] * (GLA_DK ** -0.5), eye)
        s_new = a_col * s0_ref[0, h] + k_col * v_ref[0][:, vc]
        s_ref[0, h] = s_new
        o_ref[0, :, vc] = jnp.sum(q_col * s_new, axis=0, keepdims=True)


def gla_sample(qk, v, glow, wgu, bg, s0):
    n_seq = qk.shape[0]
    st_spec = pl.BlockSpec((1, GLA_HEADS, GLA_DK, GLA_DV), lambda b: (b, 0, 0, 0))
    return pl.pallas_call(
        _gla_sample_kernel,
        grid=(n_seq,),
        in_specs=[pl.BlockSpec((1, 1, GLA_KW), lambda b: (b, 0, 0)),
                  pl.BlockSpec((1, 1, GLA_KW), lambda b: (b, 0, 1)),
                  pl.BlockSpec((1, 1, GLA_VW), lambda b: (b, 0, 0)),
                  pl.BlockSpec((1, 1, LANE), lambda b: (b, 0, 0)),
                  pl.BlockSpec((LANE, GLA_KW), lambda b: (0, 0)),
                  pl.BlockSpec((1, GLA_KW), lambda b: (0, 0)),
                  st_spec],
        out_specs=[pl.BlockSpec((1, 1, GLA_VW), lambda b: (b, 0, 0)), st_spec],
        out_shape=[jax.ShapeDtypeStruct((n_seq, 1, GLA_VW), F32),
                   jax.ShapeDtypeStruct(s0.shape, F32)],
        compiler_params=_params("parallel"),
        name="gla_sample",
    )(qk, qk, v, glow, wgu, bg, s0)


def _input_projections(x, norm_g, w_main, w_glow, tm):
    xn = rms_norm_bf16(x, norm_g, tm)
    outs = [project(xn, w_main, c, ATT_WIDTH, tm) for c in range(MAIN_WIDTH // ATT_WIDTH)]
    outs.append(project(xn, w_glow, 0, LANE, tm))
    return outs


def kernel(x_prompt, x_sample, cache_k, cache_v, state_gla, page_table, attn_norm_g, w_in,
           w_gate_up, b_gate, att_out_norm_g, gla_norm_g, w_o, ffn_norm_g, w_ffn_gate,
           w_ffn_up, w_ffn_down, final_norm_g):
    depth = w_in.shape[0]
    assert depth == 1, "single-layer trunk"
    batch, seq, d = x_prompt.shape
    n_seq, dec_seq, _ = x_sample.shape
    assert dec_seq == 1 and seq % MOBA_BLOCK == 0 and seq % GLA_CHUNK == 0
    n_pages = page_table.shape[1]
    assert (n_pages * PAGE_SIZE) % MOBA_BLOCK == 0
    n_blocks = n_pages * PAGE_SIZE // MOBA_BLOCK
    assert n_blocks >= MOBA_TOPK
    assert seq // MOBA_BLOCK == SUBLANE, "prompt MoBA packs one block-bias row per sublane"

    slopes = jnp.exp2(-8.0 * jnp.arange(1, ATT_HEADS + 1, dtype=F32) / ATT_HEADS)
    w_main = w_in[0].astype(BF16)
    w_glow = jnp.pad(w_in[0][:, MAIN_WIDTH:].astype(BF16), ((0, 0), (0, LANE - GLA_GATE_RANK)))
    wgu = jnp.pad(w_gate_up[0], ((0, LANE - GLA_GATE_RANK), (0, 0)))
    bg = b_gate[0].reshape(1, GLA_KW)
    w_o_b = w_o[0].astype(BF16)
    w_fg = w_ffn_gate[0].astype(BF16)
    w_fu = w_ffn_up[0].astype(BF16)
    w_fd = w_ffn_down[0].astype(BF16)

    xp = x_prompt.reshape(batch * seq, d)
    q_a, k_a, v_a, qk_g, v_g, r_g, glow = _input_projections(xp, attn_norm_g[0], w_main, w_glow, 512)
    att = moba_prompt(q_a, k_a, v_a, slopes, batch, seq)
    gla_o, st_p = gla_prompt(qk_g, v_g, glow, wgu, bg, batch, seq)
    h_p = out_project(xp, att, gla_o, r_g, att_out_norm_g[0], gla_norm_g[0], w_o_b, 256)
    pt_flat = page_table.reshape(-1)
    y_p, key_sums = ffn_final_norm_with_page_sums(h_p, ffn_norm_g[0], w_fg, w_fu, w_fd, final_norm_g,
                                                  cache_k, pt_flat, 512, 512)

    xs = x_sample.reshape(n_seq, d)
    q_s, k_s, v_s, qk_s, vg_s, r_s, glow_s = _input_projections(xs, attn_norm_g[0], w_main, w_glow, n_seq)
    tok3 = lambda a: a.reshape(n_seq, 1, a.shape[-1])
    heads3 = lambda a: a.reshape(n_seq, ATT_HEADS, ATT_HD)
    sel = block_select(key_sums, heads3(q_s), n_blocks)
    att_s = moba_sample(heads3(q_s), heads3(k_s), heads3(v_s), cache_k, cache_v, sel.reshape(-1),
                        pt_flat, slopes, n_pages)
    gla_s, st_s = gla_sample(tok3(qk_s), tok3(vg_s), tok3(glow_s), wgu, bg, state_gla[0])
    h_s = out_project(xs, att_s.reshape(n_seq, ATT_WIDTH), gla_s.reshape(n_seq, GLA_VW), r_s,
                      att_out_norm_g[0], gla_norm_g[0], w_o_b, n_seq)
    y_s = ffn_final_norm(h_s, ffn_norm_g[0], w_fg, w_fu, w_fd, final_norm_g, n_seq, 512)

    kv_p = (depth, batch, seq, ATT_HEADS, ATT_HD)
    kv_s = (depth, n_seq, dec_seq, ATT_HEADS, ATT_HD)
    return (y_p.reshape(batch, seq, d), y_s.reshape(n_seq, dec_seq, d),
            k_a.reshape(kv_p), v_a.reshape(kv_p), k_s.reshape(kv_s), v_s.reshape(kv_s),
            st_p[None], st_s[None])
```

```python
import functools

import jax
import jax.numpy as jnp
from jax import lax
from jax.experimental import pallas as pl
from jax.experimental.pallas import tpu as pltpu

D_MODEL = 2048
PAGE_SIZE = 128
ATT_HEADS = 8
ATT_HD = 128
ATT_WIDTH = ATT_HEADS * ATT_HD
MOBA_BLOCK = 256
MOBA_TOPK = 3
GLA_HEADS = 4
GLA_DK = 128
GLA_DV = 256
GLA_KW = GLA_HEADS * GLA_DK
GLA_VW = GLA_HEADS * GLA_DV
GLA_GATE_RANK = 16
GLA_TAU = 16.0
GLA_CHUNK = 64
MAIN_WIDTH = 3 * ATT_WIDTH + 2 * GLA_KW + 2 * GLA_VW
EPS = 1e-6

LANE = 128
SUBLANE = 8
VMEM_LIMIT = 48 * 1024 * 1024
VMEM_LIMIT_FFN_PAGES = 56 * 1024 * 1024
GLA_GROUP = 8
MOBA_HEADS_PER_STEP = 2

F32 = jnp.float32
BF16 = jnp.bfloat16
NEG_INF = float("-inf")
LOG2E = 1.4426950408889634
MASKED = -1e30


def _params(*sem):
    return pltpu.CompilerParams(dimension_semantics=sem, vmem_limit_bytes=VMEM_LIMIT)


def _dot(a, b, precision=None):
    return jnp.dot(a, b, preferred_element_type=F32, precision=precision)


def _dot_nt(a, b, precision=None):
    return lax.dot_general(a, b, (((1,), (1,)), ((), ())),
                           preferred_element_type=F32, precision=precision)


def _dot_tn(a, b, precision=None):
    return lax.dot_general(a, b, (((0,), (0,)), ((), ())),
                           preferred_element_type=F32, precision=precision)


def _split2(x):
    hi = x.astype(BF16)
    return hi, (x - hi.astype(F32)).astype(BF16)


def _split3(x):
    hi = x.astype(BF16)
    r = x - hi.astype(F32)
    mid = r.astype(BF16)
    return hi, mid, (r - mid.astype(F32)).astype(BF16)


def _dot_x3(a, b):
    a_hi, a_lo = _split2(a)
    b_hi, b_lo = _split2(b)
    return _dot(a_hi, b_hi) + (_dot(a_hi, b_lo) + _dot(a_lo, b_hi))


def _rms(x, g):
    return x * lax.rsqrt(jnp.mean(x * x, axis=-1, keepdims=True) + EPS) * g


def _silu(x):
    return x * (1.0 / (1.0 + jnp.exp(-x)))


def _log_sigmoid(x):
    return jnp.minimum(x, 0.0) - jnp.log1p(jnp.exp(-jnp.abs(x)))


def _norm_proj_kernel(x_ref, g_ref, w_ref, o_ref, xn_ref):
    xn = _rms(x_ref[...], g_ref[...]).astype(BF16)
    xn_ref[...] = xn
    o_ref[...] = _dot(xn, w_ref[...])


def norm_project(x, g, w, col_block, tn, tm):
    m, d = x.shape
    return pl.pallas_call(
        _norm_proj_kernel,
        grid=(m // tm,),
        in_specs=[pl.BlockSpec((tm, d), lambda i: (i, 0)),
                  pl.BlockSpec((1, d), lambda i: (0, 0)),
                  pl.BlockSpec((d, tn), lambda i: (0, col_block))],
        out_specs=[pl.BlockSpec((tm, tn), lambda i: (i, 0)),
                   pl.BlockSpec((tm, d), lambda i: (i, 0))],
        out_shape=[jax.ShapeDtypeStruct((m, tn), F32), jax.ShapeDtypeStruct((m, d), BF16)],
        compiler_params=_params("parallel"),
        name="norm_in_proj",
    )(x, g.reshape(1, d), w)


def _proj_kernel(x_ref, *refs):
    n = len(refs) // 2
    x = x_ref[...]
    for w_ref, o_ref in zip(refs[:n], refs[n:]):
        o_ref[...] = _dot(x, w_ref[...])


def project(xn, weights, tm):
    m, k = xn.shape
    return pl.pallas_call(
        _proj_kernel,
        grid=(m // tm,),
        in_specs=[pl.BlockSpec((tm, k), lambda i: (i, 0))]
                 + [pl.BlockSpec((k, tn), functools.partial(lambda c, i: (0, c), c)) for _, c, tn in weights],
        out_specs=[pl.BlockSpec((tm, tn), lambda i: (i, 0)) for _, _, tn in weights],
        out_shape=[jax.ShapeDtypeStruct((m, tn), F32) for _, _, tn in weights],
        compiler_params=_params("parallel"),
        name="in_proj",
    )(xn, *[w for w, _, _ in weights])


def _out_proj_kernel(x_ref, att_ref, gla_ref, r_ref, ag_ref, gg_ref, w_ref, o_ref):
    parts = [_rms(att_ref[...], ag_ref[...]).astype(BF16)]
    for h in range(GLA_HEADS):
        cols = slice(h * GLA_DV, (h + 1) * GLA_DV)
        y = _rms(gla_ref[:, cols], gg_ref[...]) * _silu(r_ref[:, cols])
        parts.append(y.astype(BF16))
    y = jnp.concatenate(parts, axis=-1)
    o_ref[...] = x_ref[...] + _dot(y, w_ref[...])


def out_project(x, att, gla_o, r_g, att_g, gla_g, w_o, tm):
    m, d = x.shape
    row = lambda i: (i, 0)
    fixed = lambda i: (0, 0)
    return pl.pallas_call(
        _out_proj_kernel,
        grid=(m // tm,),
        in_specs=[pl.BlockSpec((tm, d), row),
                  pl.BlockSpec((tm, ATT_WIDTH), row),
                  pl.BlockSpec((tm, GLA_VW), row),
                  pl.BlockSpec((tm, GLA_VW), row),
                  pl.BlockSpec((1, ATT_WIDTH), fixed),
                  pl.BlockSpec((1, GLA_DV), fixed),
                  pl.BlockSpec(w_o.shape, fixed)],
        out_specs=pl.BlockSpec((tm, d), row),
        out_shape=jax.ShapeDtypeStruct((m, d), F32),
        compiler_params=_params("parallel"),
        name="out_proj",
    )(x, att, gla_o, r_g, att_g.reshape(1, -1), gla_g.reshape(1, -1), w_o)


def _ffn_start(h_ref, g_ref, o_ref, hn_scr):
    @pl.when(pl.program_id(1) == 0)
    def _():
        hn_scr[...] = _rms(h_ref[...], g_ref[...]).astype(BF16)
        o_ref[...] = jnp.zeros_like(o_ref)


def _ffn_step(wg_ref, wu_ref, wd_ref, o_ref, hn_scr):
    hn = hn_scr[...]
    act = _silu(_dot(hn, wg_ref[...])) * _dot(hn, wu_ref[...])
    o_ref[...] += _dot(act.astype(BF16), wd_ref[...])


def _ffn_finish(h_ref, fg_ref, o_ref):
    @pl.when(pl.program_id(1) == pl.num_programs(1) - 1)
    def _():
        o_ref[...] = _rms(h_ref[...] + o_ref[...], fg_ref[...])


def _ffn_kernel(h_ref, g_ref, wg_ref, wu_ref, wd_ref, fg_ref, o_ref, hn_scr):
    _ffn_start(h_ref, g_ref, o_ref, hn_scr)
    _ffn_step(wg_ref, wu_ref, wd_ref, o_ref, hn_scr)
    _ffn_finish(h_ref, fg_ref, o_ref)


def _ffn_specs(m, d, d_ff, tm, tf):
    in_specs = [pl.BlockSpec((tm, d), lambda i, f, *_: (i, 0), pipeline_mode=pl.Buffered(1)),
                pl.BlockSpec((1, d), lambda i, f, *_: (0, 0)),
                pl.BlockSpec((d, tf), lambda i, f, *_: (0, f)),
                pl.BlockSpec((d, tf), lambda i, f, *_: (0, f)),
                pl.BlockSpec((tf, d), lambda i, f, *_: (f, 0)),
                pl.BlockSpec((1, d), lambda i, f, *_: (0, 0))]
    out_spec = pl.BlockSpec((tm, d), lambda i, f, *_: (i, 0))
    scratch = [pltpu.VMEM((tm, d), BF16)]
    return (m // tm, d_ff // tf), in_specs, out_spec, scratch


def ffn_final_norm(h, ffn_g, w_gate, w_up, w_down, final_g, tm, tf):
    m, d = h.shape
    grid, in_specs, out_spec, scratch = _ffn_specs(m, d, w_gate.shape[1], tm, tf)
    return pl.pallas_call(
        _ffn_kernel,
        grid=grid,
        in_specs=in_specs,
        out_specs=out_spec,
        out_shape=jax.ShapeDtypeStruct((m, d), F32),
        scratch_shapes=scratch,
        compiler_params=_params("parallel", "arbitrary"),
        name="ffn",
    )(h, ffn_g.reshape(1, d), w_gate, w_up, w_down, final_g.reshape(1, d))


def _ffn_pages_kernel(pt_ref, h_ref, g_ref, wg_ref, wu_ref, wd_ref, fg_ref, ck_ref, o_ref, sums_ref,
                      hn_scr, pbuf, sems, *, pages_per_step):
    n_f = pl.num_programs(1)
    t = pl.program_id(0) * n_f + pl.program_id(1)

    def fetch(step, slot):
        return [pltpu.make_async_copy(ck_ref.at[0, pt_ref[step * pages_per_step + r]],
                                      pbuf.at[slot, r], sems.at[slot])
                for r in range(pages_per_step)]

    @pl.when(t == 0)
    def _():
        for cp in fetch(0, 0):
            cp.start()

    @pl.when(t + 1 < pl.num_programs(0) * n_f)
    def _():
        for cp in fetch(t + 1, (t + 1) % 2):
            cp.start()

    slot = t % 2
    for cp in fetch(t, slot):
        cp.wait()

    _ffn_start(h_ref, g_ref, o_ref, hn_scr)
    per_block = MOBA_BLOCK // PAGE_SIZE
    for j in range(pages_per_step // per_block):
        tot = jnp.sum(pbuf[slot, per_block * j], axis=0)
        for r in range(1, per_block):
            tot = tot + jnp.sum(pbuf[slot, per_block * j + r], axis=0)
        sums_ref[j] = tot
    _ffn_step(wg_ref, wu_ref, wd_ref, o_ref, hn_scr)
    _ffn_finish(h_ref, fg_ref, o_ref)


def ffn_final_norm_with_page_sums(h, ffn_g, w_gate, w_up, w_down, final_g, cache_k, page_table_flat,
                                  tm, tf):
    m, d = h.shape
    grid, in_specs, out_spec, scratch = _ffn_specs(m, d, w_gate.shape[1], tm, tf)
    n_steps = grid[0] * grid[1]
    per_block = MOBA_BLOCK // PAGE_SIZE
    n_pages = page_table_flat.shape[0]
    pages_per_step = per_block * (-(-n_pages // (per_block * n_steps)))
    blocks_per_step = pages_per_step // per_block
    pt = jnp.pad(page_table_flat, (0, n_steps * pages_per_step - n_pages), mode="edge")
    page_shape = cache_k.shape[2:]
    return pl.pallas_call(
        functools.partial(_ffn_pages_kernel, pages_per_step=pages_per_step),
        grid_spec=pltpu.PrefetchScalarGridSpec(
            num_scalar_prefetch=1,
            grid=grid,
            in_specs=in_specs + [pl.BlockSpec(memory_space=pl.ANY)],
            out_specs=[out_spec,
                       pl.BlockSpec((blocks_per_step,) + page_shape[1:],
                                    lambda i, f, pt: (i * grid[1] + f, 0, 0))],
            scratch_shapes=scratch + [pltpu.VMEM((2, pages_per_step) + page_shape, F32),
                                      pltpu.SemaphoreType.DMA((2,))],
        ),
        out_shape=[jax.ShapeDtypeStruct((m, d), F32),
                   jax.ShapeDtypeStruct((n_steps * blocks_per_step,) + page_shape[1:], F32)],
        compiler_params=pltpu.CompilerParams(dimension_semantics=("arbitrary", "arbitrary"),
                                             vmem_limit_bytes=VMEM_LIMIT_FFN_PAGES),
        name="ffn_pages",
    )(pt, h, ffn_g.reshape(1, d), w_gate, w_up, w_down, final_g.reshape(1, d), cache_k)


def _moba_prompt_kernel(slopes_ref, q_ref, k_ref, v_ref, o_ref):
    seq = q_ref.shape[0]
    nb = seq // MOBA_BLOCK
    row = lax.broadcasted_iota(jnp.int32, (seq, LANE), 0)
    lane = lax.broadcasted_iota(jnp.int32, (seq, LANE), 1)
    block_onehot = jnp.where(lane == lax.shift_right_logical(row, MOBA_BLOCK.bit_length() - 1), 1.0, 0.0)
    block_onehot = block_onehot.astype(BF16)
    kpos = row.astype(F32)
    blk_id = lax.broadcasted_iota(jnp.int32, (nb, MOBA_BLOCK), 0)
    ext_row = lax.broadcasted_iota(jnp.int32, (LANE, MOBA_BLOCK), 0)
    ones_rows = jnp.where((ext_row >= nb) & (ext_row < nb + 3), 1.0, 0.0)
    key_loc = lax.broadcasted_iota(jnp.int32, (MOBA_BLOCK, MOBA_BLOCK), 0)
    qry_loc = lax.broadcasted_iota(jnp.int32, (MOBA_BLOCK, MOBA_BLOCK), 1)

    def head_stream(j):
        cols = slice(j * ATT_HD, (j + 1) * ATT_HD)
        slope2 = slopes_ref[pl.program_id(1) * MOBA_HEADS_PER_STEP + j] * LOG2E
        k = k_ref[:, cols]
        means = [jnp.sum(k[n * MOBA_BLOCK:(n + 1) * MOBA_BLOCK], axis=0, keepdims=True) / MOBA_BLOCK
                 for n in range(nb)]
        mean_hi, mean_lo = _split2(jnp.concatenate(means, axis=0))
        b_hi, b_mid, b_lo = _split3(slope2 * kpos)
        ext = jnp.where(lane == nb, b_hi, block_onehot)
        ext = jnp.where(lane == nb + 1, b_mid, ext)
        ext = jnp.where(lane == nb + 2, b_lo, ext)
        k_ext = jnp.concatenate([k.astype(BF16), ext], axis=-1)
        v_t = jnp.transpose(v_ref[:, cols]).astype(BF16)

        def scores(i):
            q = q_ref[i * MOBA_BLOCK:(i + 1) * MOBA_BLOCK, cols] * (ATT_HD ** -0.5 * LOG2E)
            q_t = jnp.transpose(q)
            q_ext = ones_rows
            if i > MOBA_TOPK:
                qt_hi, qt_lo = _split2(q_t)
                gate_t = _dot(mean_hi, qt_hi) + (_dot(mean_hi, qt_lo) + _dot(mean_lo, qt_hi))
                rank = jnp.zeros((nb, MOBA_BLOCK), jnp.int32)
                for m in range(i):
                    gm = gate_t[m:m + 1, :]
                    beats = (gm > gate_t) | ((gm == gate_t) & (blk_id > m))
                    rank = rank + beats.astype(jnp.int32)
                bias_t = jnp.where((rank < MOBA_TOPK) | (blk_id >= i), 0.0, MASKED)
                bias_t = jnp.concatenate([bias_t, jnp.zeros((LANE - nb, MOBA_BLOCK), F32)], axis=0)
                q_ext = bias_t + ones_rows
            q_full = jnp.concatenate([q_t.astype(BF16), q_ext.astype(BF16)], axis=0)
            return _dot(k_ext[:(i + 1) * MOBA_BLOCK], q_full)

        def attend(i, s_t):
            own = jnp.where(key_loc <= qry_loc, s_t[i * MOBA_BLOCK:], NEG_INF)
            s_t = own if i == 0 else jnp.concatenate([s_t[:i * MOBA_BLOCK], own], axis=0)
            m_row = jnp.max(s_t, axis=0, keepdims=True)
            p = jnp.exp2(s_t - m_row)
            l_row = jnp.sum(p, axis=0, keepdims=True)
            o_t = _dot(v_t[:, :(i + 1) * MOBA_BLOCK], p.astype(BF16))
            o_ref[i * MOBA_BLOCK:(i + 1) * MOBA_BLOCK, cols] = jnp.transpose(o_t * (1.0 / l_row))

        return scores, attend

    streams = [head_stream(j) for j in range(MOBA_HEADS_PER_STEP)]
    s_next = [scores(0) for scores, _ in streams]
    for i in range(nb):
        s_cur = s_next
        if i + 1 < nb:
            s_next = [scores(i + 1) for scores, _ in streams]
        for (_, attend), s_t in zip(streams, s_cur):
            attend(i, s_t)


def moba_prompt(q, k, v, slopes, batch, seq):
    spec = pl.BlockSpec((seq, MOBA_HEADS_PER_STEP * ATT_HD), lambda b, h: (b, h))
    return pl.pallas_call(
        _moba_prompt_kernel,
        grid=(batch, ATT_HEADS // MOBA_HEADS_PER_STEP),
        in_specs=[pl.BlockSpec(memory_space=pltpu.SMEM), spec, spec, spec],
        out_specs=spec,
        out_shape=jax.ShapeDtypeStruct(q.shape, F32),
        compiler_params=_params("parallel", "parallel"),
        name="moba_prompt",
    )(slopes, q, k, v)


def _gla_prompt_kernel(q_ref, k_ref, v_ref, glow_ref, wgu_ref, bg_ref, o_ref, s_ref,
                       g_scr, st_scr):
    seq = q_ref.shape[0]
    c = GLA_CHUNK
    x = _dot_x3(glow_ref[...], wgu_ref[...]) + bg_ref[...]
    g_scr[...] = _log_sigmoid(x) / GLA_TAU
    st_scr[...] = jnp.zeros_like(st_scr)
    r_i = lax.broadcasted_iota(jnp.int32, (c, c), 0)
    c_i = lax.broadcasted_iota(jnp.int32, (c, c), 1)
    causal = c_i <= r_i
    tril = jnp.where(causal, 1.0, 0.0).astype(BF16)

    def group(gi, carry):
        rows = [pl.ds(pl.multiple_of((gi * GLA_GROUP + j) * c, c), c) for j in range(GLA_GROUP)]
        cums = []
        for r in rows:
            g_hi, g_mid, g_lo = _split3(g_scr[r, :])
            cums.append(_dot(tril, g_hi) + (_dot(tril, g_mid) + _dot(tril, g_lo)))
        work = []
        for r, b in zip(rows, cums):
            q = q_ref[r, :] * (GLA_DK ** -0.5)
            k = k_ref[r, :]
            v = v_ref[r, :].astype(BF16)
            b_mid = b[c // 2:c // 2 + 1, :]
            b_last = b[c - 1:c, :]
            q_in = (q * jnp.exp(b)).astype(BF16)
            q_mid = (q * jnp.exp(b - b_mid)).astype(BF16)
            k_mid = (k * jnp.exp(b_mid - b)).astype(BF16)
            k_out = (k * jnp.exp(b_last - b)).astype(BF16)
            att = jnp.where(causal, _dot_nt(q_mid, k_mid), 0.0).astype(BF16)
            work.append((r, q_in, att, v, jnp.exp(b_last), _dot_tn(v, k_out)))
        st = st_scr[...]
        for r, q_in, att, v, decay, update in work:
            o_ref[r, :] = _dot_nt(q_in, st.astype(BF16)) + _dot(att, v)
            st = st * decay + update
        st_scr[...] = st
        return carry

    lax.fori_loop(0, seq // (c * GLA_GROUP), group, 0)
    s_ref[0, 0] = jnp.transpose(st_scr[...])


def gla_prompt(qk, v, glow, wgu, bg, batch, seq):
    return pl.pallas_call(
        _gla_prompt_kernel,
        grid=(batch, GLA_HEADS),
        in_specs=[pl.BlockSpec((seq, GLA_DK), lambda b, h: (b, h)),
                  pl.BlockSpec((seq, GLA_DK), lambda b, h: (b, GLA_HEADS + h)),
                  pl.BlockSpec((seq, GLA_DV), lambda b, h: (b, h)),
                  pl.BlockSpec((seq, LANE), lambda b, h: (b, 0)),
                  pl.BlockSpec((LANE, GLA_DK), lambda b, h: (0, h)),
                  pl.BlockSpec((1, GLA_DK), lambda b, h: (0, h))],
        out_specs=[pl.BlockSpec((seq, GLA_DV), lambda b, h: (b, h)),
                   pl.BlockSpec((1, 1, GLA_DK, GLA_DV), lambda b, h: (b, h, 0, 0))],
        out_shape=[jax.ShapeDtypeStruct((batch * seq, GLA_VW), F32),
                   jax.ShapeDtypeStruct((batch, GLA_HEADS, GLA_DK, GLA_DV), F32)],
        scratch_shapes=[pltpu.VMEM((seq, GLA_DK), F32), pltpu.VMEM((GLA_DV, GLA_DK), F32)],
        compiler_params=_params("parallel", "parallel"),
        name="gla_prompt",
    )(qk, qk, v, glow, wgu, bg)


def _block_select_kernel(sums_ref, q_ref, idx_ref):
    nb = sums_ref.shape[0]
    q = q_ref[0] * (ATT_HD ** -0.5)
    gate = jnp.sum(sums_ref[...] * q[None], axis=-1) / MOBA_BLOCK
    blk = lax.broadcasted_iota(jnp.int32, (nb, ATT_HEADS), 0)
    for j in range(MOBA_TOPK):
        best = jnp.max(gate, axis=0, keepdims=True)
        idx = jnp.min(jnp.where(gate == best, blk, nb), axis=0, keepdims=True)
        idx_ref[0, j:j + 1, :] = idx
        gate = jnp.where(blk == idx, NEG_INF, gate)


def block_select(sums, q_s, n_blocks):
    n_seq = q_s.shape[0]
    return pl.pallas_call(
        _block_select_kernel,
        grid=(n_seq,),
        in_specs=[pl.BlockSpec((n_blocks, ATT_HEADS, ATT_HD), lambda b: (b, 0, 0)),
                  pl.BlockSpec((1, ATT_HEADS, ATT_HD), lambda b: (b, 0, 0))],
        out_specs=pl.BlockSpec((1, MOBA_TOPK, ATT_HEADS), lambda b: (b, 0, 0)),
        out_shape=jax.ShapeDtypeStruct((n_seq, MOBA_TOPK, ATT_HEADS), jnp.int32),
        compiler_params=_params("parallel"),
        name="block_select",
    )(sums, q_s)


def _moba_sample_kernel(sel_ref, pt_ref, slopes_ref, q_ref, kn_ref, vn_ref, ck_ref, cv_ref, o_ref,
                        kbuf, vbuf, sems, *, n_pages):
    per_block = MOBA_BLOCK // PAGE_SIZE
    past = n_pages * PAGE_SIZE
    b = pl.program_id(0)

    def gather(seq, slot):
        copies = []
        for h in range(ATT_HEADS):
            for j in range(MOBA_TOPK):
                blk = sel_ref[(seq * MOBA_TOPK + j) * ATT_HEADS + h]
                for half in range(per_block):
                    page = pt_ref[seq * n_pages + blk * per_block + half]
                    rows = pl.ds((j * per_block + half) * PAGE_SIZE, PAGE_SIZE)
                    copies.append(pltpu.make_async_copy(
                        ck_ref.at[0, page, :, h, :], kbuf.at[slot, h, rows, :], sems.at[0, slot]))
                    copies.append(pltpu.make_async_copy(
                        cv_ref.at[0, page, :, h, :], vbuf.at[slot, h, rows, :], sems.at[1, slot]))
        return copies

    @pl.when(b == 0)
    def _():
        for cp in gather(0, 0):
            cp.start()

    @pl.when(b + 1 < pl.num_programs(0))
    def _():
        for cp in gather(b + 1, (b + 1) % 2):
            cp.start()

    slot = b % 2
    for cp in gather(b, slot):
        cp.wait()

    row = lax.broadcasted_iota(jnp.int32, (MOBA_BLOCK, 1), 0)
    for h in range(ATT_HEADS):
        q = q_ref[0, h:h + 1, :] * (ATT_HD ** -0.5)
        s = jnp.sum(kbuf[slot, h] * q, axis=-1, keepdims=True)
        dist = [(past - sel_ref[(b * MOBA_TOPK + j) * ATT_HEADS + h] * MOBA_BLOCK - row).astype(F32)
                for j in range(MOBA_TOPK)]
        s = s - slopes_ref[h] * jnp.concatenate(dist, axis=0)
        s_own = jnp.sum(q * kn_ref[0, h:h + 1, :], axis=-1, keepdims=True)
        m = jnp.maximum(jnp.max(s, axis=0, keepdims=True), s_own)
        p = jnp.exp(s - m)
        p_own = jnp.exp(s_own - m)
        denom = jnp.sum(p, axis=0, keepdims=True) + p_own
        acc = jnp.sum(p * vbuf[slot, h], axis=0, keepdims=True) + p_own * vn_ref[0, h:h + 1, :]
        o_ref[0, h:h + 1, :] = acc / denom


def moba_sample(q_s, k_new, v_new, cache_k, cache_v, sel_flat, page_table_flat, slopes, n_pages):
    n_seq = q_s.shape[0]
    tok = pl.BlockSpec((1, ATT_HEADS, ATT_HD), lambda b, sel, pt: (b, 0, 0))
    buf = pltpu.VMEM((2, ATT_HEADS, MOBA_TOPK * MOBA_BLOCK, ATT_HD), F32)
    return pl.pallas_call(
        functools.partial(_moba_sample_kernel, n_pages=n_pages),
        grid_spec=pltpu.PrefetchScalarGridSpec(
            num_scalar_prefetch=2,
            grid=(n_seq,),
            in_specs=[pl.BlockSpec(memory_space=pltpu.SMEM), tok, tok, tok,
                      pl.BlockSpec(memory_space=pl.ANY), pl.BlockSpec(memory_space=pl.ANY)],
            out_specs=tok,
            scratch_shapes=[buf, buf, pltpu.SemaphoreType.DMA((2, 2))],
        ),
        out_shape=jax.ShapeDtypeStruct(q_s.shape, F32),
        compiler_params=_params("arbitrary"),
        name="moba_sample",
    )(sel_flat, page_table_flat, slopes, q_s, k_new, v_new, cache_k, cache_v)


def _to_column(row, eye):
    return jnp.sum(jnp.where(eye, row, 0.0), axis=-1, keepdims=True)


def _gla_sample_kernel(q_ref, k_ref, v_ref, glow_ref, wgu_ref, bg_ref, s0_ref, o_ref, s_ref):
    eye = (lax.broadcasted_iota(jnp.int32, (GLA_DK, GLA_DK), 0)
           == lax.broadcasted_iota(jnp.int32, (GLA_DK, GLA_DK), 1))
    glow8 = jnp.broadcast_to(glow_ref[0], (8, LANE))
    x = _dot_x3(glow8, wgu_ref[...])[0:1, :] + bg_ref[...]
    a = jnp.exp(_log_sigmoid(x) / GLA_TAU)
    for h in range(GLA_HEADS):
        kc = slice(h * GLA_DK, (h + 1) * GLA_DK)
        vc = slice(h * GLA_DV, (h + 1) * GLA_DV)
        a_col = _to_column(a[:, kc], eye)
        k_col = _to_column(k_ref[0][:, kc], eye)
        q_col = _to_column(q_ref[0][:, kc] * (GLA_DK ** -0.5), eye)
        s_new = a_col * s0_ref[0, h] + k_col * v_ref[0][:, vc]
        s_ref[0, h] = s_new
        o_ref[0, :, vc] = jnp.sum(q_col * s_new, axis=0, keepdims=True)


def gla_sample(qk, v, glow, wgu, bg, s0):
    n_seq = qk.shape[0]
    st_spec = pl.BlockSpec((1, GLA_HEADS, GLA_DK, GLA_DV), lambda b: (b, 0, 0, 0))
    return pl.pallas_call(
        _gla_sample_kernel,
        grid=(n_seq,),
        in_specs=[pl.BlockSpec((1, 1, GLA_KW), lambda b: (b, 0, 0)),
                  pl.BlockSpec((1, 1, GLA_KW), lambda b: (b, 0, 1)),
                  pl.BlockSpec((1, 1, GLA_VW), lambda b: (b, 0, 0)),
                  pl.BlockSpec((1, 1, LANE), lambda b: (b, 0, 0)),
                  pl.BlockSpec((LANE, GLA_KW), lambda b: (0, 0)),
                  pl.BlockSpec((1, GLA_KW), lambda b: (0, 0)),
                  st_spec],
        out_specs=[pl.BlockSpec((1, 1, GLA_VW), lambda b: (b, 0, 0)), st_spec],
        out_shape=[jax.ShapeDtypeStruct((n_seq, 1, GLA_VW), F32),
                   jax.ShapeDtypeStruct(s0.shape, F32)],
        compiler_params=_params("parallel"),
        name="gla_sample",
    )(qk, qk, v, glow, wgu, bg, s0)


def _input_projections(x, norm_g, w_main, w_glow, tm):
    q_a, xn = norm_project(x, norm_g, w_main, 0, ATT_WIDTH, tm)
    k_a, = project(xn, [(w_main, 1, ATT_WIDTH)], tm)
    v_a, = project(xn, [(w_main, 2, ATT_WIDTH)], tm)
    qk_g, glow = project(xn, [(w_main, 3, ATT_WIDTH), (w_glow, 0, LANE)], tm)
    v_g, = project(xn, [(w_main, 4, ATT_WIDTH)], tm)
    r_g, = project(xn, [(w_main, 5, ATT_WIDTH)], tm)
    return q_a, k_a, v_a, qk_g, v_g, r_g, glow


def kernel(x_prompt, x_sample, cache_k, cache_v, state_gla, page_table, attn_norm_g, w_in,
           w_gate_up, b_gate, att_out_norm_g, gla_norm_g, w_o, ffn_norm_g, w_ffn_gate,
           w_ffn_up, w_ffn_down, final_norm_g):
    depth = w_in.shape[0]
    assert depth == 1, "single-layer trunk"
    batch, seq, d = x_prompt.shape
    n_seq, dec_seq, _ = x_sample.shape
    assert dec_seq == 1 and seq % MOBA_BLOCK == 0 and seq % GLA_CHUNK == 0
    n_pages = page_table.shape[1]
    assert (n_pages * PAGE_SIZE) % MOBA_BLOCK == 0
    n_blocks = n_pages * PAGE_SIZE // MOBA_BLOCK
    assert n_blocks >= MOBA_TOPK
    assert seq // MOBA_BLOCK == SUBLANE, "prompt MoBA packs one block-bias row per sublane"

    slopes = jnp.exp2(-8.0 * jnp.arange(1, ATT_HEADS + 1, dtype=F32) / ATT_HEADS)
    w_main = w_in[0].astype(BF16)
    w_glow = jnp.pad(w_in[0][:, MAIN_WIDTH:].astype(BF16), ((0, 0), (0, LANE - GLA_GATE_RANK)))
    wgu = jnp.pad(w_gate_up[0], ((0, LANE - GLA_GATE_RANK), (0, 0)))
    bg = b_gate[0].reshape(1, GLA_KW)
    w_o_b = w_o[0].astype(BF16)
    w_fg = w_ffn_gate[0].astype(BF16)
    w_fu = w_ffn_up[0].astype(BF16)
    w_fd = w_ffn_down[0].astype(BF16)

    xp = x_prompt.reshape(batch * seq, d)
    q_a, k_a, v_a, qk_g, v_g, r_g, glow = _input_projections(xp, attn_norm_g[0], w_main, w_glow, 512)
    att = moba_prompt(q_a, k_a, v_a, slopes, batch, seq)
    gla_o, st_p = gla_prompt(qk_g, v_g, glow, wgu, bg, batch, seq)
    h_p = out_project(xp, att, gla_o, r_g, att_out_norm_g[0], gla_norm_g[0], w_o_b, 256)
    pt_flat = page_table.reshape(-1)
    y_p, key_sums = ffn_final_norm_with_page_sums(h_p, ffn_norm_g[0], w_fg, w_fu, w_fd, final_norm_g,
                                                  cache_k, pt_flat, 1024, 256)

    xs = x_sample.reshape(n_seq, d)
    q_s, k_s, v_s, qk_s, vg_s, r_s, glow_s = _input_projections(xs, attn_norm_g[0], w_main, w_glow, n_seq)
    tok3 = lambda a: a.reshape(n_seq, 1, a.shape[-1])
    heads3 = lambda a: a.reshape(n_seq, ATT_HEADS, ATT_HD)
    sel = block_select(key_sums, heads3(q_s), n_blocks)
    att_s = moba_sample(heads3(q_s), heads3(k_s), heads3(v_s), cache_k, cache_v, sel.reshape(-1),
                        pt_flat, slopes, n_pages)
    gla_s, st_s = gla_sample(tok3(qk_s), tok3(vg_s), tok3(glow_s), wgu, bg, state_gla[0])
    h_s = out_project(xs, att_s.reshape(n_seq, ATT_WIDTH), gla_s.reshape(n_seq, GLA_VW), r_s,
                      att_out_norm_g[0], gla_norm_g[0], w_o_b, n_seq)
    y_s = ffn_final_norm(h_s, ffn_norm_g[0], w_fg, w_fu, w_fd, final_norm_g, n_seq, 512)

    kv_p = (depth, batch, seq, ATT_HEADS, ATT_HD)
    kv_s = (depth, n_seq, dec_seq, ATT_HEADS, ATT_HD)
    return (y_p.reshape(batch, seq, d), y_s.reshape(n_seq, dec_seq, d),
            k_a.reshape(kv_p), v_a.reshape(kv_p), k_s.reshape(kv_s), v_s.reshape(kv_s),
            st_p[None], st_s[None])
```

```python
import functools

import jax
import jax.numpy as jnp
from jax import lax
from jax.experimental import pallas as pl
from jax.experimental.pallas import tpu as pltpu

D_MODEL = 2048
PAGE_SIZE = 128
ATT_HEADS = 8
ATT_HD = 128
ATT_WIDTH = ATT_HEADS * ATT_HD
MOBA_BLOCK = 256
MOBA_TOPK = 3
GLA_HEADS = 4
GLA_DK = 128
GLA_DV = 256
GLA_KW = GLA_HEADS * GLA_DK
GLA_VW = GLA_HEADS * GLA_DV
GLA_GATE_RANK = 16
GLA_TAU = 16.0
GLA_CHUNK = 64
MAIN_WIDTH = 3 * ATT_WIDTH + 2 * GLA_KW + 2 * GLA_VW
EPS = 1e-6

LANE = 128
SUBLANE = 8
VMEM_LIMIT = 48 * 1024 * 1024
VMEM_LIMIT_FFN_PAGES = 56 * 1024 * 1024
GLA_GROUP = 8
MOBA_HEADS_PER_STEP = 2
SAMPLE_SEQS_PER_STEP = 4

F32 = jnp.float32
BF16 = jnp.bfloat16
NEG_INF = float("-inf")
LOG2E = 1.4426950408889634
MASKED = -1e30


def _params(*sem):
    return pltpu.CompilerParams(dimension_semantics=sem, vmem_limit_bytes=VMEM_LIMIT)


def _dot(a, b, precision=None):
    return jnp.dot(a, b, preferred_element_type=F32, precision=precision)


def _dot_nt(a, b, precision=None):
    return lax.dot_general(a, b, (((1,), (1,)), ((), ())),
                           preferred_element_type=F32, precision=precision)


def _dot_tn(a, b, precision=None):
    return lax.dot_general(a, b, (((0,), (0,)), ((), ())),
                           preferred_element_type=F32, precision=precision)


def _split2(x):
    hi = x.astype(BF16)
    return hi, (x - hi.astype(F32)).astype(BF16)


def _split3(x):
    hi = x.astype(BF16)
    r = x - hi.astype(F32)
    mid = r.astype(BF16)
    return hi, mid, (r - mid.astype(F32)).astype(BF16)


def _dot_x3(a, b):
    a_hi, a_lo = _split2(a)
    b_hi, b_lo = _split2(b)
    return _dot(a_hi, b_hi) + (_dot(a_hi, b_lo) + _dot(a_lo, b_hi))


def _rms(x, g):
    return x * lax.rsqrt(jnp.mean(x * x, axis=-1, keepdims=True) + EPS) * g


def _silu(x):
    return x * (1.0 / (1.0 + jnp.exp(-x)))


def _log_sigmoid(x):
    return jnp.minimum(x, 0.0) - jnp.log1p(jnp.exp(-jnp.abs(x)))


def _norm_proj_kernel(x_ref, g_ref, w_ref, o_ref, xn_ref):
    xn = _rms(x_ref[...], g_ref[...]).astype(BF16)
    xn_ref[...] = xn
    o_ref[...] = _dot(xn, w_ref[...])


def norm_project(x, g, w, col_block, tn, tm):
    m, d = x.shape
    return pl.pallas_call(
        _norm_proj_kernel,
        grid=(m // tm,),
        in_specs=[pl.BlockSpec((tm, d), lambda i: (i, 0)),
                  pl.BlockSpec((1, d), lambda i: (0, 0)),
                  pl.BlockSpec((d, tn), lambda i: (0, col_block))],
        out_specs=[pl.BlockSpec((tm, tn), lambda i: (i, 0)),
                   pl.BlockSpec((tm, d), lambda i: (i, 0))],
        out_shape=[jax.ShapeDtypeStruct((m, tn), F32), jax.ShapeDtypeStruct((m, d), BF16)],
        compiler_params=_params("parallel"),
        name="norm_in_proj",
    )(x, g.reshape(1, d), w)


def _proj_kernel(x_ref, *refs):
    n = len(refs) // 2
    x = x_ref[...]
    for w_ref, o_ref in zip(refs[:n], refs[n:]):
        o_ref[...] = _dot(x, w_ref[...]).astype(o_ref.dtype)


def project(xn, weights, tm, out_dtype=F32):
    m, k = xn.shape
    return pl.pallas_call(
        _proj_kernel,
        grid=(m // tm,),
        in_specs=[pl.BlockSpec((tm, k), lambda i: (i, 0))]
                 + [pl.BlockSpec((k, tn), functools.partial(lambda c, i: (0, c), c)) for _, c, tn in weights],
        out_specs=[pl.BlockSpec((tm, tn), lambda i: (i, 0)) for _, _, tn in weights],
        out_shape=[jax.ShapeDtypeStruct((m, tn), out_dtype) for _, _, tn in weights],
        compiler_params=_params("parallel"),
        name="in_proj",
    )(xn, *[w for w, _, _ in weights])


def _out_proj_kernel(x_ref, att_ref, gla_ref, r_ref, ag_ref, gg_ref, w_ref, o_ref):
    parts = [_rms(att_ref[...].astype(F32), ag_ref[...]).astype(BF16)]
    for h in range(GLA_HEADS):
        cols = slice(h * GLA_DV, (h + 1) * GLA_DV)
        y = _rms(gla_ref[:, cols].astype(F32), gg_ref[...]) * _silu(r_ref[:, cols].astype(F32))
        parts.append(y.astype(BF16))
    y = jnp.concatenate(parts, axis=-1)
    o_ref[...] = x_ref[...] + _dot(y, w_ref[...])


def out_project(x, att, gla_o, r_g, att_g, gla_g, w_o, tm):
    m, d = x.shape
    row = lambda i: (i, 0)
    fixed = lambda i: (0, 0)
    return pl.pallas_call(
        _out_proj_kernel,
        grid=(m // tm,),
        in_specs=[pl.BlockSpec((tm, d), row),
                  pl.BlockSpec((tm, ATT_WIDTH), row),
                  pl.BlockSpec((tm, GLA_VW), row),
                  pl.BlockSpec((tm, GLA_VW), row),
                  pl.BlockSpec((1, ATT_WIDTH), fixed),
                  pl.BlockSpec((1, GLA_DV), fixed),
                  pl.BlockSpec(w_o.shape, fixed)],
        out_specs=pl.BlockSpec((tm, d), row),
        out_shape=jax.ShapeDtypeStruct((m, d), F32),
        compiler_params=_params("parallel"),
        name="out_proj",
    )(x, att, gla_o, r_g, att_g.reshape(1, -1), gla_g.reshape(1, -1), w_o)


def _ffn_start(h_ref, g_ref, o_ref, hn_scr):
    @pl.when(pl.program_id(1) == 0)
    def _():
        hn_scr[...] = _rms(h_ref[...], g_ref[...]).astype(BF16)
        o_ref[...] = jnp.zeros_like(o_ref)


def _ffn_step(wg_ref, wu_ref, wd_ref, o_ref, hn_scr):
    hn = hn_scr[...]
    act = _silu(_dot(hn, wg_ref[...])) * _dot(hn, wu_ref[...])
    o_ref[...] += _dot(act.astype(BF16), wd_ref[...])


def _ffn_finish(h_ref, fg_ref, o_ref):
    @pl.when(pl.program_id(1) == pl.num_programs(1) - 1)
    def _():
        o_ref[...] = _rms(h_ref[...] + o_ref[...], fg_ref[...])


def _ffn_kernel(h_ref, g_ref, wg_ref, wu_ref, wd_ref, fg_ref, o_ref, hn_scr):
    _ffn_start(h_ref, g_ref, o_ref, hn_scr)
    _ffn_step(wg_ref, wu_ref, wd_ref, o_ref, hn_scr)
    _ffn_finish(h_ref, fg_ref, o_ref)


def _ffn_specs(m, d, d_ff, tm, tf):
    in_specs = [pl.BlockSpec((tm, d), lambda i, f, *_: (i, 0), pipeline_mode=pl.Buffered(1)),
                pl.BlockSpec((1, d), lambda i, f, *_: (0, 0)),
                pl.BlockSpec((d, tf), lambda i, f, *_: (0, f)),
                pl.BlockSpec((d, tf), lambda i, f, *_: (0, f)),
                pl.BlockSpec((tf, d), lambda i, f, *_: (f, 0)),
                pl.BlockSpec((1, d), lambda i, f, *_: (0, 0))]
    out_spec = pl.BlockSpec((tm, d), lambda i, f, *_: (i, 0))
    scratch = [pltpu.VMEM((tm, d), BF16)]
    return (m // tm, d_ff // tf), in_specs, out_spec, scratch


def ffn_final_norm(h, ffn_g, w_gate, w_up, w_down, final_g, tm, tf):
    m, d = h.shape
    grid, in_specs, out_spec, scratch = _ffn_specs(m, d, w_gate.shape[1], tm, tf)
    return pl.pallas_call(
        _ffn_kernel,
        grid=grid,
        in_specs=in_specs,
        out_specs=out_spec,
        out_shape=jax.ShapeDtypeStruct((m, d), F32),
        scratch_shapes=scratch,
        compiler_params=_params("parallel", "arbitrary"),
        name="ffn",
    )(h, ffn_g.reshape(1, d), w_gate, w_up, w_down, final_g.reshape(1, d))


def _ffn_pages_kernel(pt_ref, h_ref, g_ref, wg_ref, wu_ref, wd_ref, fg_ref, ck_ref, o_ref, sums_ref,
                      hn_scr, pbuf, sems, *, pages_per_step):
    n_f = pl.num_programs(1)
    t = pl.program_id(0) * n_f + pl.program_id(1)

    def fetch(step, slot):
        return [pltpu.make_async_copy(ck_ref.at[0, pt_ref[step * pages_per_step + r]],
                                      pbuf.at[slot, r], sems.at[slot])
                for r in range(pages_per_step)]

    @pl.when(t == 0)
    def _():
        for cp in fetch(0, 0):
            cp.start()

    @pl.when(t + 1 < pl.num_programs(0) * n_f)
    def _():
        for cp in fetch(t + 1, (t + 1) % 2):
            cp.start()

    slot = t % 2
    for cp in fetch(t, slot):
        cp.wait()

    _ffn_start(h_ref, g_ref, o_ref, hn_scr)
    per_block = MOBA_BLOCK // PAGE_SIZE
    for j in range(pages_per_step // per_block):
        tot = jnp.sum(pbuf[slot, per_block * j], axis=0)
        for r in range(1, per_block):
            tot = tot + jnp.sum(pbuf[slot, per_block * j + r], axis=0)
        sums_ref[j] = tot
    _ffn_step(wg_ref, wu_ref, wd_ref, o_ref, hn_scr)
    _ffn_finish(h_ref, fg_ref, o_ref)


def ffn_final_norm_with_page_sums(h, ffn_g, w_gate, w_up, w_down, final_g, cache_k, page_table_flat,
                                  tm, tf):
    m, d = h.shape
    grid, in_specs, out_spec, scratch = _ffn_specs(m, d, w_gate.shape[1], tm, tf)
    n_steps = grid[0] * grid[1]
    per_block = MOBA_BLOCK // PAGE_SIZE
    n_pages = page_table_flat.shape[0]
    pages_per_step = per_block * (-(-n_pages // (per_block * n_steps)))
    blocks_per_step = pages_per_step // per_block
    pt = jnp.pad(page_table_flat, (0, n_steps * pages_per_step - n_pages), mode="edge")
    page_shape = cache_k.shape[2:]
    return pl.pallas_call(
        functools.partial(_ffn_pages_kernel, pages_per_step=pages_per_step),
        grid_spec=pltpu.PrefetchScalarGridSpec(
            num_scalar_prefetch=1,
            grid=grid,
            in_specs=in_specs + [pl.BlockSpec(memory_space=pl.ANY)],
            out_specs=[out_spec,
                       pl.BlockSpec((blocks_per_step,) + page_shape[1:],
                                    lambda i, f, pt: (i * grid[1] + f, 0, 0))],
            scratch_shapes=scratch + [pltpu.VMEM((2, pages_per_step) + page_shape, F32),
                                      pltpu.SemaphoreType.DMA((2,))],
        ),
        out_shape=[jax.ShapeDtypeStruct((m, d), F32),
                   jax.ShapeDtypeStruct((n_steps * blocks_per_step,) + page_shape[1:], F32)],
        compiler_params=pltpu.CompilerParams(dimension_semantics=("arbitrary", "arbitrary"),
                                             vmem_limit_bytes=VMEM_LIMIT_FFN_PAGES),
        name="ffn_pages",
    )(pt, h, ffn_g.reshape(1, d), w_gate, w_up, w_down, final_g.reshape(1, d), cache_k)


def _moba_prompt_kernel(slopes_ref, q_ref, k_ref, v_ref, o_ref):
    seq = q_ref.shape[0]
    nb = seq // MOBA_BLOCK
    row = lax.broadcasted_iota(jnp.int32, (seq, LANE), 0)
    lane = lax.broadcasted_iota(jnp.int32, (seq, LANE), 1)
    block_onehot = jnp.where(lane == lax.shift_right_logical(row, MOBA_BLOCK.bit_length() - 1), 1.0, 0.0)
    block_onehot = block_onehot.astype(BF16)
    kpos = row.astype(F32)
    blk_id = lax.broadcasted_iota(jnp.int32, (nb, MOBA_BLOCK), 0)
    ext_row = lax.broadcasted_iota(jnp.int32, (LANE, MOBA_BLOCK), 0)
    ones_rows = jnp.where((ext_row >= nb) & (ext_row < nb + 3), 1.0, 0.0)
    key_loc = lax.broadcasted_iota(jnp.int32, (MOBA_BLOCK, MOBA_BLOCK), 0)
    qry_loc = lax.broadcasted_iota(jnp.int32, (MOBA_BLOCK, MOBA_BLOCK), 1)

    def head_stream(j):
        cols = slice(j * ATT_HD, (j + 1) * ATT_HD)
        slope2 = slopes_ref[pl.program_id(1) * MOBA_HEADS_PER_STEP + j] * LOG2E
        k = k_ref[:, cols]
        means = [jnp.sum(k[n * MOBA_BLOCK:(n + 1) * MOBA_BLOCK], axis=0, keepdims=True) / MOBA_BLOCK
                 for n in range(nb)]
        mean_hi, mean_lo = _split2(jnp.concatenate(means, axis=0))
        b_hi, b_mid, b_lo = _split3(slope2 * kpos)
        ext = jnp.where(lane == nb, b_hi, block_onehot)
        ext = jnp.where(lane == nb + 1, b_mid, ext)
        ext = jnp.where(lane == nb + 2, b_lo, ext)
        k_ext = jnp.concatenate([k.astype(BF16), ext], axis=-1)
        v_t = jnp.transpose(v_ref[:, cols]).astype(BF16)

        def scores(i):
            q = q_ref[i * MOBA_BLOCK:(i + 1) * MOBA_BLOCK, cols] * (ATT_HD ** -0.5 * LOG2E)
            q_t = jnp.transpose(q)
            q_ext = ones_rows
            if i > MOBA_TOPK:
                qt_hi, qt_lo = _split2(q_t)
                gate_t = _dot(mean_hi, qt_hi) + (_dot(mean_hi, qt_lo) + _dot(mean_lo, qt_hi))
                rank = jnp.zeros((nb, MOBA_BLOCK), jnp.int32)
                for m in range(i):
                    gm = gate_t[m:m + 1, :]
                    beats = (gm > gate_t) | ((gm == gate_t) & (blk_id > m))
                    rank = rank + beats.astype(jnp.int32)
                bias_t = jnp.where((rank < MOBA_TOPK) | (blk_id >= i), 0.0, MASKED)
                bias_t = jnp.concatenate([bias_t, jnp.zeros((LANE - nb, MOBA_BLOCK), F32)], axis=0)
                q_ext = bias_t + ones_rows
            q_full = jnp.concatenate([q_t.astype(BF16), q_ext.astype(BF16)], axis=0)
            return _dot(k_ext[:(i + 1) * MOBA_BLOCK], q_full)

        def attend(i, s_t):
            own = jnp.where(key_loc <= qry_loc, s_t[i * MOBA_BLOCK:], NEG_INF)
            s_t = own if i == 0 else jnp.concatenate([s_t[:i * MOBA_BLOCK], own], axis=0)
            m_row = jnp.max(s_t, axis=0, keepdims=True)
            p = jnp.exp2(s_t - m_row)
            l_row = jnp.sum(p, axis=0, keepdims=True)
            o_t = _dot(v_t[:, :(i + 1) * MOBA_BLOCK], p.astype(BF16))
            o = jnp.transpose(o_t * (1.0 / l_row))
            o_ref[i * MOBA_BLOCK:(i + 1) * MOBA_BLOCK, cols] = o.astype(o_ref.dtype)

        return scores, attend

    streams = [head_stream(j) for j in range(MOBA_HEADS_PER_STEP)]
    s_next = [scores(0) for scores, _ in streams]
    for i in range(nb):
        s_cur = s_next
        if i + 1 < nb:
            s_next = [scores(i + 1) for scores, _ in streams]
        for (_, attend), s_t in zip(streams, s_cur):
            attend(i, s_t)


def moba_prompt(q, k, v, slopes, batch, seq):
    spec = pl.BlockSpec((seq, MOBA_HEADS_PER_STEP * ATT_HD), lambda b, h: (b, h))
    return pl.pallas_call(
        _moba_prompt_kernel,
        grid=(batch, ATT_HEADS // MOBA_HEADS_PER_STEP),
        in_specs=[pl.BlockSpec(memory_space=pltpu.SMEM), spec, spec, spec],
        out_specs=spec,
        out_shape=jax.ShapeDtypeStruct(q.shape, BF16),
        compiler_params=_params("parallel", "parallel"),
        name="moba_prompt",
    )(slopes, q, k, v)


def _gla_prompt_kernel(q_ref, k_ref, v_ref, glow_ref, wgu_ref, bg_ref, o_ref, s_ref,
                       g_scr, st_scr):
    seq = q_ref.shape[0]
    c = GLA_CHUNK
    x = _dot_x3(glow_ref[...], wgu_ref[...]) + bg_ref[...]
    g_scr[...] = _log_sigmoid(x) / GLA_TAU
    st_scr[...] = jnp.zeros_like(st_scr)
    r_i = lax.broadcasted_iota(jnp.int32, (c, c), 0)
    c_i = lax.broadcasted_iota(jnp.int32, (c, c), 1)
    causal = c_i <= r_i
    tril = jnp.where(causal, 1.0, 0.0).astype(BF16)

    def group(gi, carry):
        rows = [pl.ds(pl.multiple_of((gi * GLA_GROUP + j) * c, c), c) for j in range(GLA_GROUP)]
        cums = []
        for r in rows:
            g_hi, g_mid, g_lo = _split3(g_scr[r, :])
            cums.append(_dot(tril, g_hi) + (_dot(tril, g_mid) + _dot(tril, g_lo)))
        work = []
        for r, b in zip(rows, cums):
            q = q_ref[r, :] * (GLA_DK ** -0.5)
            k = k_ref[r, :]
            v = v_ref[r, :].astype(BF16)
            b_mid = b[c // 2:c // 2 + 1, :]
            b_last = b[c - 1:c, :]
            q_in = (q * jnp.exp(b)).astype(BF16)
            q_mid = (q * jnp.exp(b - b_mid)).astype(BF16)
            k_mid = (k * jnp.exp(b_mid - b)).astype(BF16)
            k_out = (k * jnp.exp(b_last - b)).astype(BF16)
            att = jnp.where(causal, _dot_nt(q_mid, k_mid), 0.0).astype(BF16)
            work.append((r, q_in, att, v, jnp.exp(b_last), _dot_tn(v, k_out)))
        st = st_scr[...]
        for r, q_in, att, v, decay, update in work:
            o_ref[r, :] = (_dot_nt(q_in, st.astype(BF16)) + _dot(att, v)).astype(o_ref.dtype)
            st = st * decay + update
        st_scr[...] = st
        return carry

    lax.fori_loop(0, seq // (c * GLA_GROUP), group, 0)
    s_ref[0, 0] = jnp.transpose(st_scr[...])


def gla_prompt(qk, v, glow, wgu, bg, batch, seq):
    return pl.pallas_call(
        _gla_prompt_kernel,
        grid=(batch, GLA_HEADS),
        in_specs=[pl.BlockSpec((seq, GLA_DK), lambda b, h: (b, h)),
                  pl.BlockSpec((seq, GLA_DK), lambda b, h: (b, GLA_HEADS + h)),
                  pl.BlockSpec((seq, GLA_DV), lambda b, h: (b, h)),
                  pl.BlockSpec((seq, LANE), lambda b, h: (b, 0)),
                  pl.BlockSpec((LANE, GLA_DK), lambda b, h: (0, h)),
                  pl.BlockSpec((1, GLA_DK), lambda b, h: (0, h))],
        out_specs=[pl.BlockSpec((seq, GLA_DV), lambda b, h: (b, h)),
                   pl.BlockSpec((1, 1, GLA_DK, GLA_DV), lambda b, h: (b, h, 0, 0))],
        out_shape=[jax.ShapeDtypeStruct((batch * seq, GLA_VW), BF16),
                   jax.ShapeDtypeStruct((batch, GLA_HEADS, GLA_DK, GLA_DV), F32)],
        scratch_shapes=[pltpu.VMEM((seq, GLA_DK), F32), pltpu.VMEM((GLA_DV, GLA_DK), F32)],
        compiler_params=_params("parallel", "parallel"),
        name="gla_prompt",
    )(qk, qk, v, glow, wgu, bg)


def _block_select_kernel(sums_ref, q_ref, idx_ref):
    n_here = q_ref.shape[0]
    nb = sums_ref.shape[0] // n_here
    blk = lax.broadcasted_iota(jnp.int32, (nb, ATT_HEADS), 0)
    for s in range(n_here):
        q = q_ref[s] * (ATT_HD ** -0.5)
        gate = jnp.sum(sums_ref[s * nb:(s + 1) * nb] * q[None], axis=-1) / MOBA_BLOCK
        for j in range(MOBA_TOPK):
            best = jnp.max(gate, axis=0, keepdims=True)
            idx = jnp.min(jnp.where(gate == best, blk, nb), axis=0, keepdims=True)
            idx_ref[s, j:j + 1, :] = idx
            gate = jnp.where(blk == idx, NEG_INF, gate)


def block_select(sums, q_s, n_blocks):
    n_seq = q_s.shape[0]
    per = SAMPLE_SEQS_PER_STEP
    return pl.pallas_call(
        _block_select_kernel,
        grid=(n_seq // per,),
        in_specs=[pl.BlockSpec((per * n_blocks, ATT_HEADS, ATT_HD), lambda b: (b, 0, 0)),
                  pl.BlockSpec((per, ATT_HEADS, ATT_HD), lambda b: (b, 0, 0))],
        out_specs=pl.BlockSpec((per, MOBA_TOPK, ATT_HEADS), lambda b: (b, 0, 0)),
        out_shape=jax.ShapeDtypeStruct((n_seq, MOBA_TOPK, ATT_HEADS), jnp.int32),
        compiler_params=_params("parallel"),
        name="block_select",
    )(sums, q_s)


def _moba_sample_kernel(sel_ref, pt_ref, slopes_ref, q_ref, kn_ref, vn_ref, ck_ref, cv_ref, o_ref,
                        kbuf, vbuf, sems, *, n_pages):
    per_block = MOBA_BLOCK // PAGE_SIZE
    past = n_pages * PAGE_SIZE
    b = pl.program_id(0)

    def gather(seq, slot):
        copies = []
        for h in range(ATT_HEADS):
            for j in range(MOBA_TOPK):
                blk = sel_ref[(seq * MOBA_TOPK + j) * ATT_HEADS + h]
                for half in range(per_block):
                    page = pt_ref[seq * n_pages + blk * per_block + half]
                    rows = pl.ds((j * per_block + half) * PAGE_SIZE, PAGE_SIZE)
                    copies.append(pltpu.make_async_copy(
                        ck_ref.at[0, page, :, h, :], kbuf.at[slot, h, rows, :], sems.at[0, slot]))
                    copies.append(pltpu.make_async_copy(
                        cv_ref.at[0, page, :, h, :], vbuf.at[slot, h, rows, :], sems.at[1, slot]))
        return copies

    @pl.when(b == 0)
    def _():
        for cp in gather(0, 0):
            cp.start()

    @pl.when(b + 1 < pl.num_programs(0))
    def _():
        for cp in gather(b + 1, (b + 1) % 2):
            cp.start()

    slot = b % 2
    for cp in gather(b, slot):
        cp.wait()

    row = lax.broadcasted_iota(jnp.int32, (MOBA_BLOCK, 1), 0)
    for h in range(ATT_HEADS):
        q = q_ref[0, h:h + 1, :] * (ATT_HD ** -0.5)
        s = jnp.sum(kbuf[slot, h] * q, axis=-1, keepdims=True)
        dist = [(past - sel_ref[(b * MOBA_TOPK + j) * ATT_HEADS + h] * MOBA_BLOCK - row).astype(F32)
                for j in range(MOBA_TOPK)]
        s = s - slopes_ref[h] * jnp.concatenate(dist, axis=0)
        s_own = jnp.sum(q * kn_ref[0, h:h + 1, :], axis=-1, keepdims=True)
        m = jnp.maximum(jnp.max(s, axis=0, keepdims=True), s_own)
        p = jnp.exp(s - m)
        p_own = jnp.exp(s_own - m)
        denom = jnp.sum(p, axis=0, keepdims=True) + p_own
        acc = jnp.sum(p * vbuf[slot, h], axis=0, keepdims=True) + p_own * vn_ref[0, h:h + 1, :]
        o_ref[0, h:h + 1, :] = acc / denom


def moba_sample(q_s, k_new, v_new, cache_k, cache_v, sel_flat, page_table_flat, slopes, n_pages):
    n_seq = q_s.shape[0]
    tok = pl.BlockSpec((1, ATT_HEADS, ATT_HD), lambda b, sel, pt: (b, 0, 0))
    buf = pltpu.VMEM((2, ATT_HEADS, MOBA_TOPK * MOBA_BLOCK, ATT_HD), F32)
    return pl.pallas_call(
        functools.partial(_moba_sample_kernel, n_pages=n_pages),
        grid_spec=pltpu.PrefetchScalarGridSpec(
            num_scalar_prefetch=2,
            grid=(n_seq,),
            in_specs=[pl.BlockSpec(memory_space=pltpu.SMEM), tok, tok, tok,
                      pl.BlockSpec(memory_space=pl.ANY), pl.BlockSpec(memory_space=pl.ANY)],
            out_specs=tok,
            scratch_shapes=[buf, buf, pltpu.SemaphoreType.DMA((2, 2))],
        ),
        out_shape=jax.ShapeDtypeStruct(q_s.shape, F32),
        compiler_params=_params("arbitrary"),
        name="moba_sample",
    )(sel_flat, page_table_flat, slopes, q_s, k_new, v_new, cache_k, cache_v)


def _to_column(row, eye):
    return jnp.sum(jnp.where(eye, row, 0.0), axis=-1, keepdims=True)


def _gla_sample_kernel(q_ref, k_ref, v_ref, glow_ref, wgu_ref, bg_ref, s0_ref, o_ref, s_ref):
    eye = (lax.broadcasted_iota(jnp.int32, (GLA_DK, GLA_DK), 0)
           == lax.broadcasted_iota(jnp.int32, (GLA_DK, GLA_DK), 1))
    for s in range(q_ref.shape[0]):
        glow8 = jnp.broadcast_to(glow_ref[s], (8, LANE))
        x = _dot_x3(glow8, wgu_ref[...])[0:1, :] + bg_ref[...]
        a = jnp.exp(_log_sigmoid(x) / GLA_TAU)
        for h in range(GLA_HEADS):
            kc = slice(h * GLA_DK, (h + 1) * GLA_DK)
            vc = slice(h * GLA_DV, (h + 1) * GLA_DV)
            a_col = _to_column(a[:, kc], eye)
            k_col = _to_column(k_ref[s][:, kc], eye)
            q_col = _to_column(q_ref[s][:, kc] * (GLA_DK ** -0.5), eye)
            s_new = a_col * s0_ref[s, h] + k_col * v_ref[s][:, vc]
            s_ref[s, h] = s_new
            o_ref[s, :, vc] = jnp.sum(q_col * s_new, axis=0, keepdims=True)


def gla_sample(qk, v, glow, wgu, bg, s0):
    n_seq = qk.shape[0]
    per = SAMPLE_SEQS_PER_STEP
    st_spec = pl.BlockSpec((per, GLA_HEADS, GLA_DK, GLA_DV), lambda b: (b, 0, 0, 0))
    return pl.pallas_call(
        _gla_sample_kernel,
        grid=(n_seq // per,),
        in_specs=[pl.BlockSpec((per, 1, GLA_KW), lambda b: (b, 0, 0)),
                  pl.BlockSpec((per, 1, GLA_KW), lambda b: (b, 0, 1)),
                  pl.BlockSpec((per, 1, GLA_VW), lambda b: (b, 0, 0)),
                  pl.BlockSpec((per, 1, LANE), lambda b: (b, 0, 0)),
                  pl.BlockSpec((LANE, GLA_KW), lambda b: (0, 0)),
                  pl.BlockSpec((1, GLA_KW), lambda b: (0, 0)),
                  st_spec],
        out_specs=[pl.BlockSpec((per, 1, GLA_VW), lambda b: (b, 0, 0)), st_spec],
        out_shape=[jax.ShapeDtypeStruct((n_seq, 1, GLA_VW), F32),
                   jax.ShapeDtypeStruct(s0.shape, F32)],
        compiler_params=_params("parallel"),
        name="gla_sample",
    )(qk, qk, v, glow, wgu, bg, s0)


def _input_projections(x, norm_g, w_main, w_glow, tm):
    q_a, xn = norm_project(x, norm_g, w_main, 0, ATT_WIDTH, tm)
    qk_g, glow = project(xn, [(w_main, 3, ATT_WIDTH), (w_glow, 0, LANE)], tm)
    v_g, = project(xn, [(w_main, 4, ATT_WIDTH)], tm)
    r_g, = project(xn, [(w_main, 5, ATT_WIDTH)], tm, out_dtype=BF16)
    k_a, = project(xn, [(w_main, 1, ATT_WIDTH)], tm)
    v_a, = project(xn, [(w_main, 2, ATT_WIDTH)], tm)
    return q_a, k_a, v_a, qk_g, v_g, r_g, glow


def kernel(x_prompt, x_sample, cache_k, cache_v, state_gla, page_table, attn_norm_g, w_in,
           w_gate_up, b_gate, att_out_norm_g, gla_norm_g, w_o, ffn_norm_g, w_ffn_gate,
           w_ffn_up, w_ffn_down, final_norm_g):
    depth = w_in.shape[0]
    assert depth == 1, "single-layer trunk"
    batch, seq, d = x_prompt.shape
    n_seq, dec_seq, _ = x_sample.shape
    assert dec_seq == 1 and seq % MOBA_BLOCK == 0 and seq % GLA_CHUNK == 0
    n_pages = page_table.shape[1]
    assert (n_pages * PAGE_SIZE) % MOBA_BLOCK == 0
    n_blocks = n_pages * PAGE_SIZE // MOBA_BLOCK
    assert n_blocks >= MOBA_TOPK
    assert seq // MOBA_BLOCK == SUBLANE, "prompt MoBA packs one block-bias row per sublane"

    slopes = jnp.exp2(-8.0 * jnp.arange(1, ATT_HEADS + 1, dtype=F32) / ATT_HEADS)
    w_main = w_in[0].astype(BF16)
    w_glow = jnp.pad(w_in[0][:, MAIN_WIDTH:].astype(BF16), ((0, 0), (0, LANE - GLA_GATE_RANK)))
    wgu = jnp.pad(w_gate_up[0], ((0, LANE - GLA_GATE_RANK), (0, 0)))
    bg = b_gate[0].reshape(1, GLA_KW)
    w_o_b = w_o[0].astype(BF16)
    w_fg = w_ffn_gate[0].astype(BF16)
    w_fu = w_ffn_up[0].astype(BF16)
    w_fd = w_ffn_down[0].astype(BF16)

    xp = x_prompt.reshape(batch * seq, d)
    q_a, k_a, v_a, qk_g, v_g, r_g, glow = _input_projections(xp, attn_norm_g[0], w_main, w_glow, 512)
    att = moba_prompt(q_a, k_a, v_a, slopes, batch, seq)
    gla_o, st_p = gla_prompt(qk_g, v_g, glow, wgu, bg, batch, seq)
    h_p = out_project(xp, att, gla_o, r_g, att_out_norm_g[0], gla_norm_g[0], w_o_b, 256)
    pt_flat = page_table.reshape(-1)
    y_p, key_sums = ffn_final_norm_with_page_sums(h_p, ffn_norm_g[0], w_fg, w_fu, w_fd, final_norm_g,
                                                  cache_k, pt_flat, 1024, 256)

    xs = x_sample.reshape(n_seq, d)
    q_s, k_s, v_s, qk_s, vg_s, r_s, glow_s = _input_projections(xs, attn_norm_g[0], w_main, w_glow, n_seq)
    tok3 = lambda a: a.reshape(n_seq, 1, a.shape[-1])
    heads3 = lambda a: a.reshape(n_seq, ATT_HEADS, ATT_HD)
    sel = block_select(key_sums, heads3(q_s), n_blocks)
    att_s = moba_sample(heads3(q_s), heads3(k_s), heads3(v_s), cache_k, cache_v, sel.reshape(-1),
                        pt_flat, slopes, n_pages)
    gla_s, st_s = gla_sample(tok3(qk_s), tok3(vg_s), tok3(glow_s), wgu, bg, state_gla[0])
    h_s = out_project(xs, att_s.reshape(n_seq, ATT_WIDTH), gla_s.reshape(n_seq, GLA_VW), r_s,
                      att_out_norm_g[0], gla_norm_g[0], w_o_b, n_seq)
    y_s = ffn_final_norm(h_s, ffn_norm_g[0], w_fg, w_fu, w_fd, final_norm_g, n_seq, 512)

    kv_p = (depth, batch, seq, ATT_HEADS, ATT_HD)
    kv_s = (depth, n_seq, dec_seq, ATT_HEADS, ATT_HD)
    return (y_p.reshape(batch, seq, d), y_s.reshape(n_seq, dec_seq, d),
            k_a.reshape(kv_p), v_a.reshape(kv_p), k_s.reshape(kv_s), v_s.reshape(kv_s),
            st_p[None], st_s[None])
```

```python
import functools

import jax
import jax.numpy as jnp
from jax import lax
from jax.experimental import pallas as pl
from jax.experimental.pallas import tpu as pltpu

D_MODEL = 2048
PAGE_SIZE = 128
ATT_HEADS = 8
ATT_HD = 128
ATT_WIDTH = ATT_HEADS * ATT_HD
MOBA_BLOCK = 256
MOBA_TOPK = 3
GLA_HEADS = 4
GLA_DK = 128
GLA_DV = 256
GLA_KW = GLA_HEADS * GLA_DK
GLA_VW = GLA_HEADS * GLA_DV
GLA_GATE_RANK = 16
GLA_TAU = 16.0
GLA_CHUNK = 64
MAIN_WIDTH = 3 * ATT_WIDTH + 2 * GLA_KW + 2 * GLA_VW
EPS = 1e-6

LANE = 128
SUBLANE = 8
VMEM_LIMIT = 48 * 1024 * 1024
VMEM_LIMIT_FFN_PAGES = 56 * 1024 * 1024
GLA_GROUP = 8
MOBA_HEADS_PER_STEP = 2
SAMPLE_SEQS_PER_STEP = 4

F32 = jnp.float32
BF16 = jnp.bfloat16
NEG_INF = float("-inf")
LOG2E = 1.4426950408889634
MASKED = -1e30


def _params(*sem):
    return pltpu.CompilerParams(dimension_semantics=sem, vmem_limit_bytes=VMEM_LIMIT)


def _dot(a, b, precision=None):
    return jnp.dot(a, b, preferred_element_type=F32, precision=precision)


def _dot_nt(a, b, precision=None):
    return lax.dot_general(a, b, (((1,), (1,)), ((), ())),
                           preferred_element_type=F32, precision=precision)


def _dot_tn(a, b, precision=None):
    return lax.dot_general(a, b, (((0,), (0,)), ((), ())),
                           preferred_element_type=F32, precision=precision)


def _split2(x):
    hi = x.astype(BF16)
    return hi, (x - hi.astype(F32)).astype(BF16)


def _split3(x):
    hi = x.astype(BF16)
    r = x - hi.astype(F32)
    mid = r.astype(BF16)
    return hi, mid, (r - mid.astype(F32)).astype(BF16)


def _dot_x3(a, b):
    a_hi, a_lo = _split2(a)
    b_hi, b_lo = _split2(b)
    return _dot(a_hi, b_hi) + (_dot(a_hi, b_lo) + _dot(a_lo, b_hi))


def _rms(x, g):
    return x * lax.rsqrt(jnp.mean(x * x, axis=-1, keepdims=True) + EPS) * g


def _silu(x):
    return x * (1.0 / (1.0 + jnp.exp(-x)))


def _log_sigmoid(x):
    return jnp.minimum(x, 0.0) - jnp.log1p(jnp.exp(-jnp.abs(x)))


def _cast_weights(w_refs, wb_scrs):
    @pl.when(pl.program_id(0) == 0)
    def _():
        for w_ref, wb in zip(w_refs, wb_scrs):
            wb[...] = w_ref[...].astype(BF16)


def _resident(k, tn, col_block):
    return pl.BlockSpec((k, tn), lambda i: (0, col_block), pipeline_mode=pl.Buffered(1))


def _norm_proj_kernel(x_ref, g_ref, w_ref, o_ref, xn_ref, wb):
    _cast_weights([w_ref], [wb])
    xn = _rms(x_ref[...], g_ref[...]).astype(BF16)
    xn_ref[...] = xn
    o_ref[...] = _dot(xn, wb[...])


def norm_project(x, g, w, col_block, tn, tm):
    m, d = x.shape
    return pl.pallas_call(
        _norm_proj_kernel,
        grid=(m // tm,),
        in_specs=[pl.BlockSpec((tm, d), lambda i: (i, 0)),
                  pl.BlockSpec((1, d), lambda i: (0, 0)),
                  _resident(d, tn, col_block)],
        out_specs=[pl.BlockSpec((tm, tn), lambda i: (i, 0)),
                   pl.BlockSpec((tm, d), lambda i: (i, 0))],
        out_shape=[jax.ShapeDtypeStruct((m, tn), F32), jax.ShapeDtypeStruct((m, d), BF16)],
        scratch_shapes=[pltpu.VMEM((d, tn), BF16)],
        compiler_params=_params("arbitrary"),
        name="norm_in_proj",
    )(x, g.reshape(1, d), w)


def _proj_kernel(x_ref, w_ref, o_ref, wb):
    _cast_weights([w_ref], [wb])
    o_ref[...] = _dot(x_ref[...], wb[...]).astype(o_ref.dtype)


def project(xn, w, col_block, tn, tm, out_dtype=F32):
    m, k = xn.shape
    return pl.pallas_call(
        _proj_kernel,
        grid=(m // tm,),
        in_specs=[pl.BlockSpec((tm, k), lambda i: (i, 0)), _resident(k, tn, col_block)],
        out_specs=pl.BlockSpec((tm, tn), lambda i: (i, 0)),
        out_shape=jax.ShapeDtypeStruct((m, tn), out_dtype),
        scratch_shapes=[pltpu.VMEM((k, tn), BF16)],
        compiler_params=_params("arbitrary"),
        name="in_proj",
    )(xn, w)


def _gla_proj_kernel(x_ref, w_ref, wl_ref, o_ref, gl_ref, wb, wlb):
    _cast_weights([w_ref, wl_ref], [wb, wlb])
    x = x_ref[...]
    o_ref[...] = _dot(x, wb[...])
    gl_ref[...] = _dot(x, wlb[...])


def project_gla_inputs(xn, w, col_block, w_glow, tm):
    m, k = xn.shape
    tn = 2 * GLA_KW
    return pl.pallas_call(
        _gla_proj_kernel,
        grid=(m // tm,),
        in_specs=[pl.BlockSpec((tm, k), lambda i: (i, 0)), _resident(k, tn, col_block),
                  pl.BlockSpec(w_glow.shape, lambda i: (0, 0))],
        out_specs=[pl.BlockSpec((tm, tn), lambda i: (i, 0)), pl.BlockSpec((tm, LANE), lambda i: (i, 0))],
        out_shape=[jax.ShapeDtypeStruct((m, tn), F32), jax.ShapeDtypeStruct((m, LANE), F32)],
        scratch_shapes=[pltpu.VMEM((k, tn), BF16), pltpu.VMEM(w_glow.shape, BF16)],
        compiler_params=_params("arbitrary"),
        name="in_proj_gla",
    )(xn, w, w_glow)


def _out_proj_kernel(x_ref, att_ref, gla_ref, r_ref, ag_ref, gg_ref, w_ref, o_ref):
    parts = [_rms(att_ref[...].astype(F32), ag_ref[...]).astype(BF16)]
    for h in range(GLA_HEADS):
        cols = slice(h * GLA_DV, (h + 1) * GLA_DV)
        y = _rms(gla_ref[:, cols].astype(F32), gg_ref[...]) * _silu(r_ref[:, cols].astype(F32))
        parts.append(y.astype(BF16))
    y = jnp.concatenate(parts, axis=-1)
    o_ref[...] = x_ref[...] + _dot(y, w_ref[...])


def out_project(x, att, gla_o, r_g, att_g, gla_g, w_o, tm):
    m, d = x.shape
    row = lambda i: (i, 0)
    fixed = lambda i: (0, 0)
    return pl.pallas_call(
        _out_proj_kernel,
        grid=(m // tm,),
        in_specs=[pl.BlockSpec((tm, d), row),
                  pl.BlockSpec((tm, ATT_WIDTH), row),
                  pl.BlockSpec((tm, GLA_VW), row),
                  pl.BlockSpec((tm, GLA_VW), row),
                  pl.BlockSpec((1, ATT_WIDTH), fixed),
                  pl.BlockSpec((1, GLA_DV), fixed),
                  pl.BlockSpec(w_o.shape, fixed)],
        out_specs=pl.BlockSpec((tm, d), row),
        out_shape=jax.ShapeDtypeStruct((m, d), F32),
        compiler_params=_params("parallel"),
        name="out_proj",
    )(x, att, gla_o, r_g, att_g.reshape(1, -1), gla_g.reshape(1, -1), w_o)


def _ffn_start(h_ref, g_ref, o_ref, hn_scr):
    @pl.when(pl.program_id(1) == 0)
    def _():
        hn_scr[...] = _rms(h_ref[...], g_ref[...]).astype(BF16)
        o_ref[...] = jnp.zeros_like(o_ref)


def _ffn_step(wg_ref, wu_ref, wd_ref, o_ref, hn_scr):
    hn = hn_scr[...]
    act = _silu(_dot(hn, wg_ref[...])) * _dot(hn, wu_ref[...])
    o_ref[...] += _dot(act.astype(BF16), wd_ref[...])


def _ffn_finish(h_ref, fg_ref, o_ref):
    @pl.when(pl.program_id(1) == pl.num_programs(1) - 1)
    def _():
        o_ref[...] = _rms(h_ref[...] + o_ref[...], fg_ref[...])


def _ffn_kernel(h_ref, g_ref, wg_ref, wu_ref, wd_ref, fg_ref, o_ref, hn_scr):
    _ffn_start(h_ref, g_ref, o_ref, hn_scr)
    _ffn_step(wg_ref, wu_ref, wd_ref, o_ref, hn_scr)
    _ffn_finish(h_ref, fg_ref, o_ref)


def _ffn_specs(m, d, d_ff, tm, tf):
    in_specs = [pl.BlockSpec((tm, d), lambda i, f, *_: (i, 0), pipeline_mode=pl.Buffered(1)),
                pl.BlockSpec((1, d), lambda i, f, *_: (0, 0)),
                pl.BlockSpec((d, tf), lambda i, f, *_: (0, f)),
                pl.BlockSpec((d, tf), lambda i, f, *_: (0, f)),
                pl.BlockSpec((tf, d), lambda i, f, *_: (f, 0)),
                pl.BlockSpec((1, d), lambda i, f, *_: (0, 0))]
    out_spec = pl.BlockSpec((tm, d), lambda i, f, *_: (i, 0))
    scratch = [pltpu.VMEM((tm, d), BF16)]
    return (m // tm, d_ff // tf), in_specs, out_spec, scratch


def ffn_final_norm(h, ffn_g, w_gate, w_up, w_down, final_g, tm, tf):
    m, d = h.shape
    grid, in_specs, out_spec, scratch = _ffn_specs(m, d, w_gate.shape[1], tm, tf)
    return pl.pallas_call(
        _ffn_kernel,
        grid=grid,
        in_specs=in_specs,
        out_specs=out_spec,
        out_shape=jax.ShapeDtypeStruct((m, d), F32),
        scratch_shapes=scratch,
        compiler_params=_params("parallel", "arbitrary"),
        name="ffn",
    )(h, ffn_g.reshape(1, d), w_gate, w_up, w_down, final_g.reshape(1, d))


def _ffn_pages_kernel(pt_ref, h_ref, g_ref, wg_ref, wu_ref, wd_ref, fg_ref, ck_ref, o_ref, sums_ref,
                      hn_scr, pbuf, sems, *, pages_per_step):
    n_f = pl.num_programs(1)
    t = pl.program_id(0) * n_f + pl.program_id(1)

    def fetch(step, slot):
        return [pltpu.make_async_copy(ck_ref.at[0, pt_ref[step * pages_per_step + r]],
                                      pbuf.at[slot, r], sems.at[slot])
                for r in range(pages_per_step)]

    @pl.when(t == 0)
    def _():
        for cp in fetch(0, 0):
            cp.start()

    @pl.when(t + 1 < pl.num_programs(0) * n_f)
    def _():
        for cp in fetch(t + 1, (t + 1) % 2):
            cp.start()

    slot = t % 2
    for cp in fetch(t, slot):
        cp.wait()

    _ffn_start(h_ref, g_ref, o_ref, hn_scr)
    per_block = MOBA_BLOCK // PAGE_SIZE
    for j in range(pages_per_step // per_block):
        tot = jnp.sum(pbuf[slot, per_block * j], axis=0)
        for r in range(1, per_block):
            tot = tot + jnp.sum(pbuf[slot, per_block * j + r], axis=0)
        sums_ref[j] = tot
    _ffn_step(wg_ref, wu_ref, wd_ref, o_ref, hn_scr)
    _ffn_finish(h_ref, fg_ref, o_ref)


def ffn_final_norm_with_page_sums(h, ffn_g, w_gate, w_up, w_down, final_g, cache_k, page_table_flat,
                                  tm, tf):
    m, d = h.shape
    grid, in_specs, out_spec, scratch = _ffn_specs(m, d, w_gate.shape[1], tm, tf)
    n_steps = grid[0] * grid[1]
    per_block = MOBA_BLOCK // PAGE_SIZE
    n_pages = page_table_flat.shape[0]
    pages_per_step = per_block * (-(-n_pages // (per_block * n_steps)))
    blocks_per_step = pages_per_step // per_block
    pt = jnp.pad(page_table_flat, (0, n_steps * pages_per_step - n_pages), mode="edge")
    page_shape = cache_k.shape[2:]
    return pl.pallas_call(
        functools.partial(_ffn_pages_kernel, pages_per_step=pages_per_step),
        grid_spec=pltpu.PrefetchScalarGridSpec(
            num_scalar_prefetch=1,
            grid=grid,
            in_specs=in_specs + [pl.BlockSpec(memory_space=pl.ANY)],
            out_specs=[out_spec,
                       pl.BlockSpec((blocks_per_step,) + page_shape[1:],
                                    lambda i, f, pt: (i * grid[1] + f, 0, 0))],
            scratch_shapes=scratch + [pltpu.VMEM((2, pages_per_step) + page_shape, F32),
                                      pltpu.SemaphoreType.DMA((2,))],
        ),
        out_shape=[jax.ShapeDtypeStruct((m, d), F32),
                   jax.ShapeDtypeStruct((n_steps * blocks_per_step,) + page_shape[1:], F32)],
        compiler_params=pltpu.CompilerParams(dimension_semantics=("arbitrary", "arbitrary"),
                                             vmem_limit_bytes=VMEM_LIMIT_FFN_PAGES),
        name="ffn_pages",
    )(pt, h, ffn_g.reshape(1, d), w_gate, w_up, w_down, final_g.reshape(1, d), cache_k)


def _moba_prompt_kernel(slopes_ref, q_ref, k_ref, v_ref, o_ref):
    seq = q_ref.shape[0]
    nb = seq // MOBA_BLOCK
    row = lax.broadcasted_iota(jnp.int32, (seq, LANE), 0)
    lane = lax.broadcasted_iota(jnp.int32, (seq, LANE), 1)
    block_onehot = jnp.where(lane == lax.shift_right_logical(row, MOBA_BLOCK.bit_length() - 1), 1.0, 0.0)
    block_onehot = block_onehot.astype(BF16)
    kpos = row.astype(F32)
    blk_id = lax.broadcasted_iota(jnp.int32, (nb, MOBA_BLOCK), 0)
    ext_row = lax.broadcasted_iota(jnp.int32, (LANE, MOBA_BLOCK), 0)
    ones_rows = jnp.where((ext_row >= nb) & (ext_row < nb + 3), 1.0, 0.0)
    key_loc = lax.broadcasted_iota(jnp.int32, (MOBA_BLOCK, MOBA_BLOCK), 0)
    qry_loc = lax.broadcasted_iota(jnp.int32, (MOBA_BLOCK, MOBA_BLOCK), 1)

    def head_stream(j):
        cols = slice(j * ATT_HD, (j + 1) * ATT_HD)
        slope2 = slopes_ref[pl.program_id(1) * MOBA_HEADS_PER_STEP + j] * LOG2E
        k = k_ref[:, cols]
        means = [jnp.sum(k[n * MOBA_BLOCK:(n + 1) * MOBA_BLOCK], axis=0, keepdims=True) / MOBA_BLOCK
                 for n in range(nb)]
        mean_hi, mean_lo = _split2(jnp.concatenate(means, axis=0))
        b_hi, b_mid, b_lo = _split3(slope2 * kpos)
        ext = jnp.where(lane == nb, b_hi, block_onehot)
        ext = jnp.where(lane == nb + 1, b_mid, ext)
        ext = jnp.where(lane == nb + 2, b_lo, ext)
        k_ext = jnp.concatenate([k.astype(BF16), ext], axis=-1)
        v_t = jnp.transpose(v_ref[:, cols]).astype(BF16)

        def scores(i):
            q = q_ref[i * MOBA_BLOCK:(i + 1) * MOBA_BLOCK, cols] * (ATT_HD ** -0.5 * LOG2E)
            q_t = jnp.transpose(q)
            q_ext = ones_rows
            if i > MOBA_TOPK:
                qt_hi, qt_lo = _split2(q_t)
                gate_t = _dot(mean_hi, qt_hi) + (_dot(mean_hi, qt_lo) + _dot(mean_lo, qt_hi))
                rank = jnp.zeros((nb, MOBA_BLOCK), jnp.int32)
                for m in range(i):
                    gm = gate_t[m:m + 1, :]
                    beats = (gm > gate_t) | ((gm == gate_t) & (blk_id > m))
                    rank = rank + beats.astype(jnp.int32)
                bias_t = jnp.where((rank < MOBA_TOPK) | (blk_id >= i), 0.0, MASKED)
                bias_t = jnp.concatenate([bias_t, jnp.zeros((LANE - nb, MOBA_BLOCK), F32)], axis=0)
                q_ext = bias_t + ones_rows
            q_full = jnp.concatenate([q_t.astype(BF16), q_ext.astype(BF16)], axis=0)
            return _dot(k_ext[:(i + 1) * MOBA_BLOCK], q_full)

        def attend(i, s_t):
            own = jnp.where(key_loc <= qry_loc, s_t[i * MOBA_BLOCK:], NEG_INF)
            s_t = own if i == 0 else jnp.concatenate([s_t[:i * MOBA_BLOCK], own], axis=0)
            m_row = jnp.max(s_t, axis=0, keepdims=True)
            p = jnp.exp2(s_t - m_row)
            l_row = jnp.sum(p, axis=0, keepdims=True)
            o_t = _dot(v_t[:, :(i + 1) * MOBA_BLOCK], p.astype(BF16))
            o = jnp.transpose(o_t * (1.0 / l_row))
            o_ref[i * MOBA_BLOCK:(i + 1) * MOBA_BLOCK, cols] = o.astype(o_ref.dtype)

        return scores, attend

    streams = [head_stream(j) for j in range(MOBA_HEADS_PER_STEP)]
    s_next = [scores(0) for scores, _ in streams]
    for i in range(nb):
        s_cur = s_next
        if i + 1 < nb:
            s_next = [scores(i + 1) for scores, _ in streams]
        for (_, attend), s_t in zip(streams, s_cur):
            attend(i, s_t)


def moba_prompt(q, k, v, slopes, batch, seq):
    spec = pl.BlockSpec((seq, MOBA_HEADS_PER_STEP * ATT_HD), lambda b, h: (b, h))
    return pl.pallas_call(
        _moba_prompt_kernel,
        grid=(batch, ATT_HEADS // MOBA_HEADS_PER_STEP),
        in_specs=[pl.BlockSpec(memory_space=pltpu.SMEM), spec, spec, spec],
        out_specs=spec,
        out_shape=jax.ShapeDtypeStruct(q.shape, BF16),
        compiler_params=_params("parallel", "parallel"),
        name="moba_prompt",
    )(slopes, q, k, v)


def _gla_prompt_kernel(q_ref, k_ref, v_ref, glow_ref, wgu_ref, bg_ref, o_ref, s_ref,
                       g_ref, st_scr):
    seq = q_ref.shape[0]
    c = GLA_CHUNK
    x = _dot_x3(glow_ref[...], wgu_ref[...]) + bg_ref[...]
    g_ref[...] = _log_sigmoid(x) / GLA_TAU
    st_scr[...] = jnp.zeros_like(st_scr)
    r_i = lax.broadcasted_iota(jnp.int32, (c, c), 0)
    c_i = lax.broadcasted_iota(jnp.int32, (c, c), 1)
    causal = c_i <= r_i
    tril = jnp.where(causal, 1.0, 0.0).astype(BF16)

    def group(gi, carry):
        rows = [pl.ds(pl.multiple_of((gi * GLA_GROUP + j) * c, c), c) for j in range(GLA_GROUP)]
        cums = []
        for r in rows:
            g_hi, g_mid, g_lo = _split3(g_ref[r, :])
            cums.append(_dot(tril, g_hi) + (_dot(tril, g_mid) + _dot(tril, g_lo)))
        work = []
        for r, b in zip(rows, cums):
            q = q_ref[r, :] * (GLA_DK ** -0.5)
            k = k_ref[r, :]
            v = v_ref[r, :].astype(BF16)
            b_mid = b[c // 2:c // 2 + 1, :]
            b_last = b[c - 1:c, :]
            q_in = (q * jnp.exp(b)).astype(BF16)
            q_mid = (q * jnp.exp(b - b_mid)).astype(BF16)
            k_mid = (k * jnp.exp(b_mid - b)).astype(BF16)
            k_out = (k * jnp.exp(b_last - b)).astype(BF16)
            att = jnp.where(causal, _dot_nt(q_mid, k_mid), 0.0).astype(BF16)
            work.append((r, q_in, att, v, jnp.exp(b_last), _dot_tn(v, k_out)))
        st = st_scr[...]
        for r, q_in, att, v, decay, update in work:
            o_ref[r, :] = (_dot_nt(q_in, st.astype(BF16)) + _dot(att, v)).astype(o_ref.dtype)
            st = st * decay + update
        st_scr[...] = st
        return carry

    lax.fori_loop(0, seq // (c * GLA_GROUP), group, 0)
    s_ref[0, 0] = jnp.transpose(st_scr[...])


def gla_prompt(qk, v, glow, wgu, bg, batch, seq):
    return pl.pallas_call(
        _gla_prompt_kernel,
        grid=(batch, GLA_HEADS),
        in_specs=[pl.BlockSpec((seq, GLA_DK), lambda b, h: (b, h)),
                  pl.BlockSpec((seq, GLA_DK), lambda b, h: (b, GLA_HEADS + h)),
                  pl.BlockSpec((seq, GLA_DV), lambda b, h: (b, h)),
                  pl.BlockSpec((seq, LANE), lambda b, h: (b, 0)),
                  pl.BlockSpec((LANE, GLA_DK), lambda b, h: (0, h)),
                  pl.BlockSpec((1, GLA_DK), lambda b, h: (0, h))],
        out_specs=[pl.BlockSpec((seq, GLA_DV), lambda b, h: (b, h)),
                   pl.BlockSpec((1, 1, GLA_DK, GLA_DV), lambda b, h: (b, h, 0, 0))],
        out_shape=[jax.ShapeDtypeStruct((batch * seq, GLA_VW), BF16),
                   jax.ShapeDtypeStruct((batch, GLA_HEADS, GLA_DK, GLA_DV), F32)],
        scratch_shapes=[pltpu.VMEM((seq, GLA_DK), F32), pltpu.VMEM((GLA_DV, GLA_DK), F32)],
        compiler_params=_params("parallel", "parallel"),
        name="gla_prompt",
    )(qk, qk, v, glow, wgu, bg)


def _block_select_kernel(sums_ref, q_ref, idx_ref):
    n_here = q_ref.shape[0]
    nb = sums_ref.shape[0] // n_here
    blk = lax.broadcasted_iota(jnp.int32, (nb, ATT_HEADS), 0)
    for s in range(n_here):
        q = q_ref[s] * (ATT_HD ** -0.5)
        gate = jnp.sum(sums_ref[s * nb:(s + 1) * nb] * q[None], axis=-1) / MOBA_BLOCK
        for j in range(MOBA_TOPK):
            best = jnp.max(gate, axis=0, keepdims=True)
            idx = jnp.min(jnp.where(gate == best, blk, nb), axis=0, keepdims=True)
            idx_ref[s, j:j + 1, :] = idx
            gate = jnp.where(blk == idx, NEG_INF, gate)


def block_select(sums, q_s, n_blocks):
    n_seq = q_s.shape[0]
    per = SAMPLE_SEQS_PER_STEP
    return pl.pallas_call(
        _block_select_kernel,
        grid=(n_seq // per,),
        in_specs=[pl.BlockSpec((per * n_blocks, ATT_HEADS, ATT_HD), lambda b: (b, 0, 0)),
                  pl.BlockSpec((per, ATT_HEADS, ATT_HD), lambda b: (b, 0, 0))],
        out_specs=pl.BlockSpec((per, MOBA_TOPK, ATT_HEADS), lambda b: (b, 0, 0)),
        out_shape=jax.ShapeDtypeStruct((n_seq, MOBA_TOPK, ATT_HEADS), jnp.int32),
        compiler_params=_params("parallel"),
        name="block_select",
    )(sums, q_s)


def _moba_sample_kernel(sel_ref, pt_ref, slopes_ref, q_ref, kn_ref, vn_ref, ck_ref, cv_ref, o_ref,
                        kbuf, vbuf, sems, *, n_pages):
    per_block = MOBA_BLOCK // PAGE_SIZE
    past = n_pages * PAGE_SIZE
    b = pl.program_id(0)

    def gather(seq, slot):
        copies = []
        for h in range(ATT_HEADS):
            for j in range(MOBA_TOPK):
                blk = sel_ref[(seq * MOBA_TOPK + j) * ATT_HEADS + h]
                for half in range(per_block):
                    page = pt_ref[seq * n_pages + blk * per_block + half]
                    rows = pl.ds((j * per_block + half) * PAGE_SIZE, PAGE_SIZE)
                    copies.append(pltpu.make_async_copy(
                        ck_ref.at[0, page, :, h, :], kbuf.at[slot, h, rows, :], sems.at[0, slot]))
                    copies.append(pltpu.make_async_copy(
                        cv_ref.at[0, page, :, h, :], vbuf.at[slot, h, rows, :], sems.at[1, slot]))
        return copies

    @pl.when(b == 0)
    def _():
        for cp in gather(0, 0):
            cp.start()

    @pl.when(b + 1 < pl.num_programs(0))
    def _():
        for cp in gather(b + 1, (b + 1) % 2):
            cp.start()

    slot = b % 2
    for cp in gather(b, slot):
        cp.wait()

    row = lax.broadcasted_iota(jnp.int32, (MOBA_BLOCK, 1), 0)
    for h in range(ATT_HEADS):
        q = q_ref[0, h:h + 1, :] * (ATT_HD ** -0.5)
        s = jnp.sum(kbuf[slot, h] * q, axis=-1, keepdims=True)
        dist = [(past - sel_ref[(b * MOBA_TOPK + j) * ATT_HEADS + h] * MOBA_BLOCK - row).astype(F32)
                for j in range(MOBA_TOPK)]
        s = s - slopes_ref[h] * jnp.concatenate(dist, axis=0)
        s_own = jnp.sum(q * kn_ref[0, h:h + 1, :], axis=-1, keepdims=True)
        m = jnp.maximum(jnp.max(s, axis=0, keepdims=True), s_own)
        p = jnp.exp(s - m)
        p_own = jnp.exp(s_own - m)
        denom = jnp.sum(p, axis=0, keepdims=True) + p_own
        acc = jnp.sum(p * vbuf[slot, h], axis=0, keepdims=True) + p_own * vn_ref[0, h:h + 1, :]
        o_ref[0, h:h + 1, :] = acc / denom


def moba_sample(q_s, k_new, v_new, cache_k, cache_v, sel_flat, page_table_flat, slopes, n_pages):
    n_seq = q_s.shape[0]
    tok = pl.BlockSpec((1, ATT_HEADS, ATT_HD), lambda b, sel, pt: (b, 0, 0))
    buf = pltpu.VMEM((2, ATT_HEADS, MOBA_TOPK * MOBA_BLOCK, ATT_HD), F32)
    return pl.pallas_call(
        functools.partial(_moba_sample_kernel, n_pages=n_pages),
        grid_spec=pltpu.PrefetchScalarGridSpec(
            num_scalar_prefetch=2,
            grid=(n_seq,),
            in_specs=[pl.BlockSpec(memory_space=pltpu.SMEM), tok, tok, tok,
                      pl.BlockSpec(memory_space=pl.ANY), pl.BlockSpec(memory_space=pl.ANY)],
            out_specs=tok,
            scratch_shapes=[buf, buf, pltpu.SemaphoreType.DMA((2, 2))],
        ),
        out_shape=jax.ShapeDtypeStruct(q_s.shape, F32),
        compiler_params=_params("arbitrary"),
        name="moba_sample",
    )(sel_flat, page_table_flat, slopes, q_s, k_new, v_new, cache_k, cache_v)


def _to_column(row, eye):
    return jnp.sum(jnp.where(eye, row, 0.0), axis=-1, keepdims=True)


def _gla_sample_kernel(q_ref, k_ref, v_ref, glow_ref, wgu_ref, bg_ref, s0_ref, o_ref, s_ref):
    eye = (lax.broadcasted_iota(jnp.int32, (GLA_DK, GLA_DK), 0)
           == lax.broadcasted_iota(jnp.int32, (GLA_DK, GLA_DK), 1))
    for s in range(q_ref.shape[0]):
        glow8 = jnp.broadcast_to(glow_ref[s], (8, LANE))
        x = _dot_x3(glow8, wgu_ref[...])[0:1, :] + bg_ref[...]
        a = jnp.exp(_log_sigmoid(x) / GLA_TAU)
        for h in range(GLA_HEADS):
            kc = slice(h * GLA_DK, (h + 1) * GLA_DK)
            vc = slice(h * GLA_DV, (h + 1) * GLA_DV)
            a_col = _to_column(a[:, kc], eye)
            k_col = _to_column(k_ref[s][:, kc], eye)
            q_col = _to_column(q_ref[s][:, kc] * (GLA_DK ** -0.5), eye)
            s_new = a_col * s0_ref[s, h] + k_col * v_ref[s][:, vc]
            s_ref[s, h] = s_new
            o_ref[s, :, vc] = jnp.sum(q_col * s_new, axis=0, keepdims=True)


def gla_sample(qk, v, glow, wgu, bg, s0):
    n_seq = qk.shape[0]
    per = SAMPLE_SEQS_PER_STEP
    st_spec = pl.BlockSpec((per, GLA_HEADS, GLA_DK, GLA_DV), lambda b: (b, 0, 0, 0))
    return pl.pallas_call(
        _gla_sample_kernel,
        grid=(n_seq // per,),
        in_specs=[pl.BlockSpec((per, 1, GLA_KW), lambda b: (b, 0, 0)),
                  pl.BlockSpec((per, 1, GLA_KW), lambda b: (b, 0, 1)),
                  pl.BlockSpec((per, 1, GLA_VW), lambda b: (b, 0, 0)),
                  pl.BlockSpec((per, 1, LANE), lambda b: (b, 0, 0)),
                  pl.BlockSpec((LANE, GLA_KW), lambda b: (0, 0)),
                  pl.BlockSpec((1, GLA_KW), lambda b: (0, 0)),
                  st_spec],
        out_specs=[pl.BlockSpec((per, 1, GLA_VW), lambda b: (b, 0, 0)), st_spec],
        out_shape=[jax.ShapeDtypeStruct((n_seq, 1, GLA_VW), F32),
                   jax.ShapeDtypeStruct(s0.shape, F32)],
        compiler_params=_params("parallel"),
        name="gla_sample",
    )(qk, qk, v, glow, wgu, bg, s0)


def _input_projections(x, norm_g, w_main, w_glow, tm):
    q_a, xn = norm_project(x, norm_g, w_main, 0, ATT_WIDTH, tm)
    k_a = project(xn, w_main, 1, ATT_WIDTH, tm)
    v_a = project(xn, w_main, 2, ATT_WIDTH, tm)
    qk_g, glow = project_gla_inputs(xn, w_main, 3, w_glow, tm)
    v_g = project(xn, w_main, 4, ATT_WIDTH, tm)
    r_g = project(xn, w_main, 5, ATT_WIDTH, tm, out_dtype=BF16)
    return q_a, k_a, v_a, qk_g, v_g, r_g, glow


def kernel(x_prompt, x_sample, cache_k, cache_v, state_gla, page_table, attn_norm_g, w_in,
           w_gate_up, b_gate, att_out_norm_g, gla_norm_g, w_o, ffn_norm_g, w_ffn_gate,
           w_ffn_up, w_ffn_down, final_norm_g):
    depth = w_in.shape[0]
    assert depth == 1, "single-layer trunk"
    batch, seq, d = x_prompt.shape
    n_seq, dec_seq, _ = x_sample.shape
    assert dec_seq == 1 and seq % MOBA_BLOCK == 0 and seq % GLA_CHUNK == 0
    n_pages = page_table.shape[1]
    assert (n_pages * PAGE_SIZE) % MOBA_BLOCK == 0
    n_blocks = n_pages * PAGE_SIZE // MOBA_BLOCK
    assert n_blocks >= MOBA_TOPK
    assert seq // MOBA_BLOCK == SUBLANE, "prompt MoBA packs one block-bias row per sublane"

    slopes = jnp.exp2(-8.0 * jnp.arange(1, ATT_HEADS + 1, dtype=F32) / ATT_HEADS)
    w_main = w_in[0]
    w_glow = jnp.pad(w_in[0][:, MAIN_WIDTH:], ((0, 0), (0, LANE - GLA_GATE_RANK)))
    wgu = jnp.pad(w_gate_up[0], ((0, LANE - GLA_GATE_RANK), (0, 0)))
    bg = b_gate[0].reshape(1, GLA_KW)
    w_o_b = w_o[0].astype(BF16)
    w_fg = w_ffn_gate[0].astype(BF16)
    w_fu = w_ffn_up[0].astype(BF16)
    w_fd = w_ffn_down[0].astype(BF16)

    xp = x_prompt.reshape(batch * seq, d)
    q_a, k_a, v_a, qk_g, v_g, r_g, glow = _input_projections(xp, attn_norm_g[0], w_main, w_glow, 512)
    att = moba_prompt(q_a, k_a, v_a, slopes, batch, seq)
    gla_o, st_p = gla_prompt(qk_g, v_g, glow, wgu, bg, batch, seq)
    h_p = out_project(xp, att, gla_o, r_g, att_out_norm_g[0], gla_norm_g[0], w_o_b, 256)
    pt_flat = page_table.reshape(-1)
    y_p, key_sums = ffn_final_norm_with_page_sums(h_p, ffn_norm_g[0], w_fg, w_fu, w_fd, final_norm_g,
                                                  cache_k, pt_flat, 1024, 256)

    xs = x_sample.reshape(n_seq, d)
    q_s, k_s, v_s, qk_s, vg_s, r_s, glow_s = _input_projections(xs, attn_norm_g[0], w_main, w_glow, n_seq)
    tok3 = lambda a: a.reshape(n_seq, 1, a.shape[-1])
    heads3 = lambda a: a.reshape(n_seq, ATT_HEADS, ATT_HD)
    sel = block_select(key_sums, heads3(q_s), n_blocks)
    att_s = moba_sample(heads3(q_s), heads3(k_s), heads3(v_s), cache_k, cache_v, sel.reshape(-1),
                        pt_flat, slopes, n_pages)
    gla_s, st_s = gla_sample(tok3(qk_s), tok3(vg_s), tok3(glow_s), wgu, bg, state_gla[0])
    h_s = out_project(xs, att_s.reshape(n_seq, ATT_WIDTH), gla_s.reshape(n_seq, GLA_VW), r_s,
                      att_out_norm_g[0], gla_norm_g[0], w_o_b, n_seq)
    y_s = ffn_final_norm(h_s, ffn_norm_g[0], w_fg, w_fu, w_fd, final_norm_g, n_seq, 512)

    kv_p = (depth, batch, seq, ATT_HEADS, ATT_HD)
    kv_s = (depth, n_seq, dec_seq, ATT_HEADS, ATT_HD)
    return (y_p.reshape(batch, seq, d), y_s.reshape(n_seq, dec_seq, d),
            k_a.reshape(kv_p), v_a.reshape(kv_p), k_s.reshape(kv_s), v_s.reshape(kv_s),
            st_p[None], st_s[None])
```
